```python
import math
import jax, jax.numpy as jnp
from jax import lax
import numpy as np

D_MODEL = 1024
BATCH = 16
SEQ = 4096
DEPTH = 2

N_BRANCH = 4
BRANCH_W = 256
BLOCK_Q = 128
ROPE_THETA = 10000.0
EPS = 1e-6
POS_OFFSET_MAX = 1024
CONV_CH = BRANCH_W
CONV_K = 31
DIFF_VD = 64
DIFF_QK = 32
DIFF_HEADS = BRANCH_W // DIFF_VD
SB_HD = 64
SB_HEADS = BRANCH_W // SB_HD
DSA_HD = 64
DSA_HEADS = BRANCH_W // DSA_HD
IDX_HEADS = 8
IDX_HD = 32
TOPK_MAX = 256
D_FF = 2816
FFN_CONV_K = 3

SPLIT_SIZES = (
    2 * CONV_CH,
    DIFF_HEADS * 2 * DIFF_QK,
    DIFF_HEADS * 2 * DIFF_QK,
    DIFF_HEADS * DIFF_VD,
    SB_HEADS * SB_HD,
    SB_HEADS * SB_HD,
    SB_HEADS * SB_HD,
    DSA_HEADS * DSA_HD,
    DSA_HD,
    DSA_HD,
    IDX_HEADS * IDX_HD,
    IDX_HD,
    IDX_HEADS,
    N_BRANCH * D_MODEL,
)
D_IN = sum(SPLIT_SIZES)
SPLIT_POINTS = tuple(sum(SPLIT_SIZES[: i + 1]) for i in range(len(SPLIT_SIZES) - 1))

kernel_name = "hybrid_gated_four_mixer_block"


def rmsnorm(x, g):
    x32 = x.astype(jnp.float32)
    y = x32 * lax.rsqrt(jnp.mean(x32 * x32, axis=-1, keepdims=True) + EPS)
    return (y * g.astype(jnp.float32)).astype(x.dtype)


def layernorm(x, g, b):
    x32 = x.astype(jnp.float32)
    mu = jnp.mean(x32, axis=-1, keepdims=True)
    xc = x32 - mu
    y = xc * lax.rsqrt(jnp.mean(xc * xc, axis=-1, keepdims=True) + EPS)
    return (y * g.astype(jnp.float32) + b.astype(jnp.float32)).astype(x.dtype)


def rope(x, pos):
    d = x.shape[-1]
    inv_freq = ROPE_THETA ** (-jnp.arange(0, d, 2, dtype=jnp.float32) / d)
    ang = pos.astype(jnp.float32)[..., None] * inv_freq
    ang = ang.reshape(ang.shape[:2] + (1,) * (x.ndim - 3) + ang.shape[-1:])
    cos, sin = jnp.cos(ang), jnp.sin(ang)
    x1, x2 = jnp.split(x.astype(jnp.float32), 2, axis=-1)
    return jnp.concatenate([x1 * cos - x2 * sin, x2 * cos + x1 * sin], axis=-1).astype(x.dtype)


def causal_dwconv(x, w, b):
    k = w.shape[0]
    y = lax.conv_general_dilated(
        x, w[:, None, :].astype(x.dtype), window_strides=(1,), padding=[(k - 1, 0)],
        dimension_numbers=('NWC', 'WIO', 'NWC'), feature_group_count=x.shape[-1])
    return y + b.astype(x.dtype)


def conformer_conv(u, conv_w, conv_b, ln_g, ln_b):
    a, g = jnp.split(u, 2, axis=-1)
    h = a * jax.nn.sigmoid(g)
    h = causal_dwconv(h, conv_w, conv_b)
    h = layernorm(h, ln_g, ln_b)
    return jax.nn.silu(h)


def diff_attention(q, k, v, pos, lam, lam_init, subln_g):
    q = rope(q, pos)
    k = rope(k, pos)
    b, s_len = q.shape[0], q.shape[1]
    scale = DIFF_QK ** -0.5
    outs = []
    for t0 in range(0, s_len, BLOCK_Q):
        t1 = t0 + BLOCK_Q
        s = jnp.einsum('bqhmd,bkhmd->bhmqk', q[:, t0:t1], k[:, :t1]).astype(jnp.float32) * scale
        mask = jnp.arange(t1)[None, :] <= jnp.arange(t0, t1)[:, None]
        p = jax.nn.softmax(jnp.where(mask, s, -jnp.inf), axis=-1)
        a = p[:, :, 0] - lam * p[:, :, 1]
        outs.append(jnp.einsum('bhqk,bkhd->bqhd', a.astype(v.dtype), v[:, :t1]))
    o = jnp.concatenate(outs, axis=1)
    o = rmsnorm(o, subln_g) * (1.0 - lam_init)
    return o.reshape(b, s_len, -1)


def stick_breaking_attention(q, k, v):
    b, s_len = q.shape[0], q.shape[1]
    scale = SB_HD ** -0.5
    outs = []
    for t0 in range(0, s_len, BLOCK_Q):
        t1 = t0 + BLOCK_Q
        z = jnp.einsum('bqhd,bkhd->bhqk', q[:, t0:t1], k[:, :t1]).astype(jnp.float32) * scale
        strict = jnp.arange(t1)[None, :] < jnp.arange(t0, t1)[:, None]
        log_keep = jnp.where(strict, jax.nn.log_sigmoid(-z), 0.0)
        later = lax.cumsum(log_keep, axis=3, reverse=True) - log_keep
        w = jnp.where(strict, jnp.exp(jax.nn.log_sigmoid(z) + later), 0.0)
        outs.append(jnp.einsum('bhqk,bkhd->bqhd', w.astype(v.dtype), v[:, :t1]))
    o = jnp.concatenate(outs, axis=1)
    return o.reshape(b, s_len, -1)


def dsa_attention(q, k, v, qi, ki, wi, pos):
    q = rope(q, pos)
    k = rope(k, pos)
    qi = rope(qi, pos)
    ki = rope(ki, pos)
    b, s_len = q.shape[0], q.shape[1]
    topk = min(TOPK_MAX, s_len // 4)
    wi = wi.astype(jnp.float32) * IDX_HEADS ** -0.5
    scale = DSA_HD ** -0.5
    gather = jax.vmap(lambda arr, idx: arr[idx])
    outs = []
    for t0 in range(0, s_len, BLOCK_Q):
        t1 = t0 + BLOCK_Q
        kl = min(s_len, max(t1, topk))
        qpos = jnp.arange(t0, t1)
        logits = jnp.einsum('bqhd,bkd->bqhk', qi[:, t0:t1], ki[:, :kl]).astype(jnp.float32) * IDX_HD ** -0.5
        score = jnp.einsum('bqhk,bqh->bqk', jax.nn.relu(logits), wi[:, t0:t1])
        score = jnp.where(jnp.arange(kl)[None, None, :] <= qpos[None, :, None], score, -jnp.inf)
        _, idx = lax.top_k(score, topk)
        k_sel = gather(k, idx)
        v_sel = gather(v, idx)
        s = jnp.einsum('bqhd,bqkd->bqhk', q[:, t0:t1], k_sel).astype(jnp.float32) * scale
        valid = (idx <= qpos[None, :, None])[:, :, None, :]
        p = jax.nn.softmax(jnp.where(valid, s, -jnp.inf), axis=-1)
        outs.append(jnp.einsum('bqhk,bqkd->bqhd', p.astype(v.dtype), v_sel))
    o = jnp.concatenate(outs, axis=1)
    return o.reshape(b, s_len, -1)


def hybrid_mixer(h, pos, w_in, conv_w, conv_b, ln_g, ln_b, lam_q1, lam_k1, lam_q2, lam_k2,
                 lam_init, subln_g, w_branch, w_out):
    b, s_len, _ = h.shape
    z = h @ w_in
    (u_a, q_b, k_b, v_b, q_c, k_c, v_c, q_d, k_d, v_d, qi_d, ki_d, wi_d, gates) = jnp.split(
        z, SPLIT_POINTS, axis=-1)
    o_a = conformer_conv(u_a, conv_w, conv_b, ln_g, ln_b)
    f32 = jnp.float32
    lam = (jnp.exp(jnp.sum(lam_q1.astype(f32) * lam_k1.astype(f32)))
           - jnp.exp(jnp.sum(lam_q2.astype(f32) * lam_k2.astype(f32))) + lam_init)
    o_b = diff_attention(q_b.reshape(b, s_len, DIFF_HEADS, 2, DIFF_QK),
                         k_b.reshape(b, s_len, DIFF_HEADS, 2, DIFF_QK),
                         v_b.reshape(b, s_len, DIFF_HEADS, DIFF_VD), pos, lam, lam_init, subln_g)
    o_c = stick_breaking_attention(q_c.reshape(b, s_len, SB_HEADS, SB_HD),
                                   k_c.reshape(b, s_len, SB_HEADS, SB_HD),
                                   v_c.reshape(b, s_len, SB_HEADS, SB_HD))
    o_d = dsa_attention(q_d.reshape(b, s_len, DSA_HEADS, DSA_HD), k_d, v_d,
                        qi_d.reshape(b, s_len, IDX_HEADS, IDX_HD), ki_d, wi_d, pos)
    g = jax.nn.sigmoid(gates.astype(f32)).astype(h.dtype).reshape(b, s_len, N_BRANCH, D_MODEL)
    merged = g[:, :, 0] * (o_a @ w_branch[0])
    for i, o in ((1, o_b), (2, o_c), (3, o_d)):
        merged = merged + g[:, :, i] * (o @ w_branch[i])
    return merged @ w_out


def conv_ffn(h, w_up, conv_w, conv_b, w_down):
    u = causal_dwconv(h @ w_up, conv_w, conv_b)
    gate, val = jnp.split(u, 2, axis=-1)
    return (jax.nn.silu(gate) * val) @ w_down


def setup_inputs(seed: int = 0) -> dict:
    key = jax.random.key(seed)
    ks = jax.random.split(key, 26)
    f32 = jnp.float32

    def nrm(k, shape, scale):
        return jax.random.normal(k, shape, f32) * scale

    x = nrm(ks[0], (BATCH, SEQ, D_MODEL), 1.0)
    c = nrm(ks[1], (BATCH, D_MODEL), 1.0)
    positions = (jnp.arange(SEQ, dtype=jnp.int32)[None, :]
                 + jax.random.randint(ks[2], (BATCH, 1), 0, POS_OFFSET_MAX, dtype=jnp.int32))
    return {
        'x': x,
        'c': c,
        'positions': positions,
        'ada_w': nrm(ks[3], (DEPTH, D_MODEL, 6 * D_MODEL), 0.5 * D_MODEL ** -0.5),
        'ada_b': nrm(ks[4], (DEPTH, 6 * D_MODEL), 0.01),
        'mix_pre_g': 1.0 + nrm(ks[5], (DEPTH, D_MODEL), 0.05),
        'mix_post_g': 1.0 + nrm(ks[6], (DEPTH, D_MODEL), 0.05),
        'ffn_pre_g': 1.0 + nrm(ks[7], (DEPTH, D_MODEL), 0.05),
        'ffn_post_g': 1.0 + nrm(ks[8], (DEPTH, D_MODEL), 0.05),
        'w_in': nrm(ks[9], (DEPTH, D_MODEL, D_IN), D_MODEL ** -0.5),
        'conv_a_w': nrm(ks[10], (DEPTH, CONV_K, CONV_CH), CONV_K ** -0.5),
        'conv_a_b': nrm(ks[11], (DEPTH, CONV_CH), 0.01),
        'conv_a_ln_g': 1.0 + nrm(ks[12], (DEPTH, CONV_CH), 0.05),
        'conv_a_ln_b': nrm(ks[13], (DEPTH, CONV_CH), 0.01),
        'lam_q1': nrm(ks[14], (DEPTH, DIFF_QK), 0.1),
        'lam_k1': nrm(ks[15], (DEPTH, DIFF_QK), 0.1),
        'lam_q2': nrm(ks[16], (DEPTH, DIFF_QK), 0.1),
        'lam_k2': nrm(ks[17], (DEPTH, DIFF_QK), 0.1),
        'diff_subln_g': 1.0 + nrm(ks[18], (DEPTH, DIFF_VD), 0.05),
        'w_branch': nrm(ks[19], (DEPTH, N_BRANCH, BRANCH_W, D_MODEL), BRANCH_W ** -0.5),
        'w_out': nrm(ks[20], (DEPTH, D_MODEL, D_MODEL), D_MODEL ** -0.5),
        'w_up': nrm(ks[21], (DEPTH, D_MODEL, 2 * D_FF), D_MODEL ** -0.5),
        'ffn_conv_w': nrm(ks[22], (DEPTH, FFN_CONV_K, 2 * D_FF), FFN_CONV_K ** -0.5),
        'ffn_conv_b': nrm(ks[23], (DEPTH, 2 * D_FF), 0.01),
        'w_down': nrm(ks[24], (DEPTH, D_FF, D_MODEL), D_FF ** -0.5),
    }


def reference(x, c, positions, ada_w, ada_b, mix_pre_g, mix_post_g, ffn_pre_g, ffn_post_g,
              w_in, conv_a_w, conv_a_b, conv_a_ln_g, conv_a_ln_b, lam_q1, lam_k1, lam_q2, lam_k2,
              diff_subln_g, w_branch, w_out, w_up, ffn_conv_w, ffn_conv_b, w_down):
    c_act = jax.nn.silu(c)
    for l in range(DEPTH):
        lam_init = 0.8 - 0.6 * math.exp(-0.3 * l)
        mod = c_act @ ada_w[l] + ada_b[l]
        sh1, sc1, g1, sh2, sc2, g2 = [m[:, None, :] for m in jnp.split(mod, 6, axis=-1)]
        h = rmsnorm(x, mix_pre_g[l]) * (1.0 + sc1) + sh1
        y = hybrid_mixer(h, positions, w_in[l], conv_a_w[l], conv_a_b[l], conv_a_ln_g[l],
                         conv_a_ln_b[l], lam_q1[l], lam_k1[l], lam_q2[l], lam_k2[l], lam_init,
                         diff_subln_g[l], w_branch[l], w_out[l])
        x = x + g1 * rmsnorm(y, mix_post_g[l])
        h = rmsnorm(x, ffn_pre_g[l]) * (1.0 + sc2) + sh2
        y = conv_ffn(h, w_up[l], ffn_conv_w[l], ffn_conv_b[l], w_down[l])
        x = x + g2 * rmsnorm(y, ffn_post_g[l])
    return x
```

```python
import functools
import math

import jax
import jax.numpy as jnp
from jax import lax
from jax.experimental import pallas as pl
from jax.experimental.pallas import tpu as pltpu

F32 = jnp.float32
BF16 = jnp.bfloat16

N_BRANCH = 4
BRANCH_W = 256
ROPE_THETA = 10000.0
EPS = 1e-6
CONV_CH = 256
CONV_K = 31
DIFF_VD = 64
DIFF_QK = 32
DIFF_HEADS = 4
SB_HD = 64
SB_HEADS = 4
DSA_HD = 64
DSA_HEADS = 4
IDX_HEADS = 8
IDX_HD = 32
TOPK_MAX = 256
FFN_CONV_K = 3

ZMAIN_W = 2816
GATES_W = N_BRANCH * 1024
MISC_KD, MISC_VD, MISC_KI, MISC_WI = 0, 64, 128, 160

VMEM_LIMIT = 56 * 1024 * 1024
INT_MIN = -(2 ** 31)
NEG_BIG = -1e30


def _cparams(sem):
    return pltpu.CompilerParams(dimension_semantics=sem, vmem_limit_bytes=VMEM_LIMIT)


def _nt(a, b):
    return lax.dot_general(a, b, (((1,), (1,)), ((), ())), preferred_element_type=F32)


def _nn(a, b):
    return jnp.dot(a, b, preferred_element_type=F32)


def _mod_kernel(c_ref, w_ref, b_ref, o_ref):
    c = c_ref[...]
    c_act = c * jax.nn.sigmoid(c)
    o_ref[0] = _nn(c_act, w_ref[0]) + b_ref[0]


def adaln_mod(c, ada_w, ada_b):
    depth, d, n = ada_w.shape
    b = c.shape[0]
    tn = 1024
    return pl.pallas_call(
        _mod_kernel,
        grid=(depth, n // tn),
        in_specs=[pl.BlockSpec((b, d), lambda l, j: (0, 0)),
                  pl.BlockSpec((1, d, tn), lambda l, j: (l, 0, j)),
                  pl.BlockSpec((1, 1, tn), lambda l, j: (l, 0, j))],
        out_specs=pl.BlockSpec((1, b, tn), lambda l, j: (l, 0, j)),
        out_shape=jax.ShapeDtypeStruct((depth, b, n), F32),
        compiler_params=_cparams(("arbitrary", "arbitrary")),
        name="adaln_mod",
    )(c, ada_w, ada_b.reshape(depth, 1, n))


def _rope_table_kernel(pos_ref, f_ref, c32_ref, s32_ref, c64_ref, s64_ref):
    pos = pos_ref[0].astype(F32)
    f = f_ref[...]
    a32 = pos * f[0:1]
    a64 = pos * f[2:3]
    c32_ref[0] = jnp.cos(a32)
    s32_ref[0] = jnp.sin(a32) * f[1:2]
    c64_ref[0] = jnp.cos(a64)
    s64_ref[0] = jnp.sin(a64) * f[3:4]


def rope_tables(positions, ts=512):
    b, s = positions.shape
    ts = min(ts, s)

    def pattern(d):
        inv = ROPE_THETA ** (-jnp.arange(0, d, 2, dtype=F32) / d)
        freq = jnp.tile(jnp.concatenate([inv, inv]), 128 // d)
        sign = jnp.tile(jnp.concatenate([-jnp.ones(d // 2, F32), jnp.ones(d // 2, F32)]), 128 // d)
        return freq, sign

    f32_, s32_ = pattern(IDX_HD)
    f64_, s64_ = pattern(DSA_HD)
    ftab = jnp.stack([f32_, s32_, f64_, s64_])
    tab = jax.ShapeDtypeStruct((b, s, 128), F32)
    spec = pl.BlockSpec((1, ts, 128), lambda bi, i: (bi, i, 0))
    return pl.pallas_call(
        _rope_table_kernel,
        grid=(b, s // ts),
        in_specs=[pl.BlockSpec((1, ts, 1), lambda bi, i: (bi, i, 0)),
                  pl.BlockSpec((4, 128), lambda bi, i: (0, 0))],
        out_specs=[spec] * 4,
        out_shape=[tab] * 4,
        compiler_params=_cparams(("arbitrary", "arbitrary")),
        name="rope_tables",
    )(positions.reshape(b, s, 1), ftab)


def _nmm_kernel(x_ref, mod_ref, g_ref, w_ref, o_ref, *, shift_row, scale_row):
    x = x_ref[0]
    mod = mod_ref[0]
    ms = jnp.mean(x * x, axis=-1, keepdims=True)
    y = x * lax.rsqrt(ms + EPS) * g_ref[...]
    h = y * (1.0 + mod[scale_row:scale_row + 1]) + mod[shift_row:shift_row + 1]
    o_ref[0] = _nn(h.astype(BF16), w_ref[...]).astype(o_ref.dtype)


def norm_mod_matmul(x, mod6, g, w, out_dtype, shift_row, scale_row, tm=512, name="nmm"):
    b, s, d = x.shape
    n = w.shape[1]
    tm = min(tm, s)
    return pl.pallas_call(
        functools.partial(_nmm_kernel, shift_row=shift_row, scale_row=scale_row),
        grid=(b, s // tm),
        in_specs=[pl.BlockSpec((1, tm, d), lambda bi, i: (bi, i, 0)),
                  pl.BlockSpec((1, 6, d), lambda bi, i: (bi, 0, 0)),
                  pl.BlockSpec((1, d), lambda bi, i: (0, 0)),
                  pl.BlockSpec((d, n), lambda bi, i: (0, 0))],
        out_specs=pl.BlockSpec((1, tm, n), lambda bi, i: (bi, i, 0)),
        out_shape=jax.ShapeDtypeStruct((b, s, n), out_dtype),
        compiler_params=_cparams(("arbitrary", "arbitrary")),
        name=name,
    )(x, mod6, g.reshape(1, d), w)


def _rope(x, cos, sin_signed, half):
    w = x.shape[1]
    lane = lax.broadcasted_iota(jnp.int32, x.shape, 1)
    first = (lane % (2 * half)) < half
    swapped = jnp.where(first, pltpu.roll(x, w - half, 1), pltpu.roll(x, half, 1))
    return x * cos + swapped * sin_signed


def _prep_kernel(qb_ref, kb_ref, vb_ref, qc_ref, kc_ref, vc_ref, qd_ref, md_ref, qi_ref,
                 c32_ref, s32_ref, c64_ref, s64_ref,
                 oqb, okb, ovb, oqc, okc, ovc, oqd, okdki, ovd, oqi, owi):
    c32 = jnp.concatenate([c32_ref[0]] * 2, axis=1)
    s32 = jnp.concatenate([s32_ref[0]] * 2, axis=1)
    c64 = jnp.concatenate([c64_ref[0]] * 2, axis=1)
    s64 = jnp.concatenate([s64_ref[0]] * 2, axis=1)
    h32, h64 = IDX_HD // 2, DSA_HD // 2
    oqb[0] = (_rope(qb_ref[0], c32, s32, h32) * (DIFF_QK ** -0.5)).astype(BF16)
    okb[0] = _rope(kb_ref[0], c32, s32, h32).astype(BF16)
    ovb[0] = vb_ref[0].astype(BF16)
    oqc[0] = (qc_ref[0] * (SB_HD ** -0.5)).astype(BF16)
    okc[0] = kc_ref[0].astype(BF16)
    ovc[0] = vc_ref[0].astype(BF16)
    oqd[0] = (_rope(qd_ref[0], c64, s64, h64) * (DSA_HD ** -0.5)).astype(BF16)
    oqi[0] = _rope(qi_ref[0], c32, s32, h32).astype(BF16)
    md = md_ref[0]
    md64 = _rope(md, c64, s64, h64)
    md32 = _rope(md, c32, s32, h32)
    ts = md.shape[0]
    okdki[0] = jnp.concatenate(
        [md64[:, MISC_KD:MISC_KD + DSA_HD], md32[:, MISC_KI:MISC_KI + IDX_HD],
         jnp.zeros((ts, 128 - DSA_HD - IDX_HD), F32)], axis=1).astype(BF16)
    ovd[0] = jnp.concatenate(
        [md[:, MISC_VD:MISC_VD + DSA_HD], jnp.zeros((ts, 128 - DSA_HD), F32)], axis=1).astype(BF16)
    owi[0] = md[:, MISC_WI:MISC_WI + IDX_HEADS] * (IDX_HEADS ** -0.5 * IDX_HD ** -0.5)


def prep(zmain, tabs, ts=512):
    b, s, _ = zmain.shape
    ts = min(ts, s)
    zspec = lambda cb: pl.BlockSpec((1, ts, 256), lambda bi, i, cb=cb: (bi, i, cb))
    tspec = pl.BlockSpec((1, ts, 128), lambda bi, i: (bi, i, 0))
    o256 = pl.BlockSpec((1, ts, 256), lambda bi, i: (bi, i, 0))
    o128 = pl.BlockSpec((1, ts, 128), lambda bi, i: (bi, i, 0))
    o8 = pl.BlockSpec((1, ts, IDX_HEADS), lambda bi, i: (bi, i, 0))
    s256 = jax.ShapeDtypeStruct((b, s, 256), BF16)
    s128 = jax.ShapeDtypeStruct((b, s, 128), BF16)
    return pl.pallas_call(
        _prep_kernel,
        grid=(b, s // ts),
        in_specs=[zspec(cb) for cb in range(2, 11)] + [tspec] * 4,
        out_specs=[o256] * 7 + [o128, o128, o256, o8],
        out_shape=[s256] * 7 + [s128, s128, s256, jax.ShapeDtypeStruct((b, s, IDX_HEADS), F32)],
        compiler_params=_cparams(("arbitrary", "arbitrary")),
        name="prep",
    )(*([zmain] * 9), *tabs)


CONV_HALO = 32


def _conformer_kernel(u_ref, halo_ref, w_ref, b_ref, g_ref, beta_ref, o_ref, buf, *, ts):
    i = pl.program_id(1)
    u = u_ref[0]
    buf[CONV_HALO:CONV_HALO + ts, :] = u[:, :CONV_CH] * jax.nn.sigmoid(u[:, CONV_CH:])
    uh = halo_ref[0]
    hh = uh[:, :CONV_CH] * jax.nn.sigmoid(uh[:, CONV_CH:])
    buf[0:CONV_HALO, :] = jnp.where(i == 0, 0.0, hh)
    w = w_ref[...]
    acc = jnp.zeros((ts, CONV_CH), F32) + b_ref[...]
    for k in range(CONV_K):
        acc = acc + w[k:k + 1] * buf[pl.ds(CONV_HALO - (CONV_K - 1) + k, ts), :]
    mu = jnp.mean(acc, axis=-1, keepdims=True)
    xc = acc - mu
    y = xc * lax.rsqrt(jnp.mean(xc * xc, axis=-1, keepdims=True) + EPS) * g_ref[...] + beta_ref[...]
    o_ref[0] = (y * jax.nn.sigmoid(y)).astype(o_ref.dtype)


def conformer(zmain, conv_w, conv_b, ln_g, ln_b, ts=512):
    b, s, _ = zmain.shape
    ts = min(ts, s)
    r = ts // CONV_HALO
    wpad = jnp.concatenate([conv_w, jnp.zeros((32 - CONV_K, CONV_CH), F32)], axis=0)
    vec = lambda a: a.reshape(1, CONV_CH)
    vspec = pl.BlockSpec((1, CONV_CH), lambda bi, i: (0, 0))
    return pl.pallas_call(
        functools.partial(_conformer_kernel, ts=ts),
        grid=(b, s // ts),
        in_specs=[pl.BlockSpec((1, ts, 2 * CONV_CH), lambda bi, i: (bi, i, 0)),
                  pl.BlockSpec((1, CONV_HALO, 2 * CONV_CH), lambda bi, i: (bi, jnp.maximum(i * r - 1, 0), 0)),
                  pl.BlockSpec((32, CONV_CH), lambda bi, i: (0, 0)),
                  vspec, vspec, vspec],
        out_specs=pl.BlockSpec((1, ts, CONV_CH), lambda bi, i: (bi, i, 0)),
        out_shape=jax.ShapeDtypeStruct((b, s, CONV_CH), BF16),
        scratch_shapes=[pltpu.VMEM((ts + CONV_HALO, CONV_CH), F32)],
        compiler_params=_cparams(("arbitrary", "arbitrary")),
        name="conformer",
    )(zmain, zmain, wpad, vec(conv_b), vec(ln_g), vec(ln_b))


def _diff_kernel(q_ref, k_ref, v_ref, lq1, lk1, lq2, lk2, sg_ref, o_ref, m_sc, l_sc, acc_sc,
                 *, t, lam_init):
    i = pl.program_id(1)
    j = pl.program_id(2)

    @pl.when(j == 0)
    def _():
        m_sc[...] = jnp.full(m_sc.shape, -jnp.inf, F32)
        l_sc[...] = jnp.zeros(l_sc.shape, F32)
        acc_sc[...] = jnp.zeros(acc_sc.shape, F32)

    def step(masked):
        q = q_ref[0]
        k = k_ref[0]
        v = v_ref[0]
        if masked:
            row = lax.broadcasted_iota(jnp.int32, (t, t), 0)
            col = lax.broadcasted_iota(jnp.int32, (t, t), 1)
            causal = col <= row
        for h in range(DIFF_HEADS):
            vh = v[:, h * DIFF_VD:(h + 1) * DIFF_VD]
            for mm in range(2):
                jj = 2 * h + mm
                s = _nt(q[:, jj * DIFF_QK:(jj + 1) * DIFF_QK], k[:, jj * DIFF_QK:(jj + 1) * DIFF_QK])
                if masked:
                    s = jnp.where(causal, s, -jnp.inf)
                m_prev = m_sc[jj]
                m_cur = jnp.maximum(m_prev, jnp.max(s, axis=1, keepdims=True))
                alpha = jnp.exp(m_prev - m_cur)
                p = jnp.exp(s - m_cur)
                l_sc[jj] = alpha * l_sc[jj] + jnp.sum(p, axis=1, keepdims=True)
                acc_sc[jj] = alpha * acc_sc[jj] + _nn(p.astype(BF16), vh)
                m_sc[jj] = m_cur

    @pl.when(j < i)
    def _():
        step(False)

    @pl.when(j == i)
    def _():
        step(True)
        f32 = F32
        lam = (jnp.exp(jnp.sum(lq1[...].astype(f32) * lk1[...].astype(f32), axis=1, keepdims=True))
               - jnp.exp(jnp.sum(lq2[...].astype(f32) * lk2[...].astype(f32), axis=1, keepdims=True))
               + lam_init)
        outs = []
        for h in range(DIFF_HEADS):
            o0 = acc_sc[2 * h] / l_sc[2 * h]
            o1 = acc_sc[2 * h + 1] / l_sc[2 * h + 1]
            o = o0 - lam * o1
            o = o * lax.rsqrt(jnp.mean(o * o, axis=-1, keepdims=True) + EPS) * sg_ref[...]
            outs.append(o * (1.0 - lam_init))
        o_ref[0] = jnp.concatenate(outs, axis=1).astype(o_ref.dtype)


def diff_attention(qb, kb, vb, lq1, lk1, lq2, lk2, subln_g, lam_init, t=512):
    b, s, w = qb.shape
    t = min(t, s)
    n = s // t
    qspec = pl.BlockSpec((1, t, w), lambda bi, i, j: (bi, i, 0))
    kspec = pl.BlockSpec((1, t, w), lambda bi, i, j: (bi, jnp.minimum(j, i), 0))
    pspec = pl.BlockSpec((1, DIFF_QK), lambda bi, i, j: (0, 0))
    nmaps = 2 * DIFF_HEADS
    return pl.pallas_call(
        functools.partial(_diff_kernel, t=t, lam_init=lam_init),
        grid=(b, n, n),
        in_specs=[qspec, kspec, kspec, pspec, pspec, pspec, pspec,
                  pl.BlockSpec((1, DIFF_VD), lambda bi, i, j: (0, 0))],
        out_specs=pl.BlockSpec((1, t, w), lambda bi, i, j: (bi, i, 0)),
        out_shape=jax.ShapeDtypeStruct((b, s, w), BF16),
        scratch_shapes=[pltpu.VMEM((nmaps, t, 1), F32), pltpu.VMEM((nmaps, t, 1), F32),
                        pltpu.VMEM((nmaps, t, DIFF_VD), F32)],
        compiler_params=_cparams(("arbitrary", "arbitrary", "arbitrary")),
        name="diff_attention",
    )(qb, kb, vb, lq1.reshape(1, -1), lk1.reshape(1, -1), lq2.reshape(1, -1), lk2.reshape(1, -1),
      subln_g.reshape(1, -1))


def _sb_kernel(q_ref, k_ref, v_ref, o_ref, c_sc, acc_sc, *, t):
    i = pl.program_id(1)
    j = pl.program_id(2)

    @pl.when(j == 0)
    def _():
        c_sc[...] = jnp.zeros(c_sc.shape, F32)
        acc_sc[...] = jnp.zeros(acc_sc.shape, F32)

    def step(masked):
        q = q_ref[0]
        k = k_ref[0]
        v = v_ref[0]
        row = lax.broadcasted_iota(jnp.int32, (t, t), 0)
        col = lax.broadcasted_iota(jnp.int32, (t, t), 1)
        tri = jnp.where(row > col, 1.0, 0.0).astype(BF16)
        strict = col < row
        for h in range(SB_HEADS):
            sl = slice(h * SB_HD, (h + 1) * SB_HD)
            z = _nt(q[:, sl], k[:, sl])
            ls_pos = jnp.minimum(z, 0.0) - jnp.log1p(jnp.exp(-jnp.abs(z)))
            keep = ls_pos - z
            if masked:
                keep = jnp.where(strict, keep, 0.0)
            hi = keep.astype(BF16)
            lo = (keep - hi.astype(F32)).astype(BF16)
            later = _nn(hi, tri) + _nn(lo, tri) + c_sc[h]
            wgt = jnp.exp(ls_pos + later)
            if masked:
                wgt = jnp.where(strict, wgt, 0.0)
            acc_sc[h] = acc_sc[h] + _nn(wgt.astype(BF16), v[:, sl])
            c_sc[h] = c_sc[h] + jnp.sum(keep, axis=1, keepdims=True)

    @pl.when(j == 0)
    def _():
        step(True)

    @pl.when(jnp.logical_and(j > 0, j <= i))
    def _():
        step(False)

    @pl.when(j == i)
    def _():
        o_ref[0] = jnp.concatenate([acc_sc[h] for h in range(SB_HEADS)], axis=1).astype(o_ref.dtype)


def sb_attention(qc, kc, vc, t=512):
    b, s, w = qc.shape
    t = min(t, s)
    n = s // t
    qspec = pl.BlockSpec((1, t, w), lambda bi, i, j: (bi, i, 0))
    kspec = pl.BlockSpec((1, t, w), lambda bi, i, j: (bi, jnp.maximum(i - j, 0), 0))
    return pl.pallas_call(
        functools.partial(_sb_kernel, t=t),
        grid=(b, n, n),
        in_specs=[qspec, kspec, kspec],
        out_specs=pl.BlockSpec((1, t, w), lambda bi, i, j: (bi, i, 0)),
        out_shape=jax.ShapeDtypeStruct((b, s, w), BF16),
        scratch_shapes=[pltpu.VMEM((SB_HEADS, t, 1), F32), pltpu.VMEM((SB_HEADS, t, SB_HD), F32)],
        compiler_params=_cparams(("arbitrary", "arbitrary", "arbitrary")),
        name="sb_attention",
    )(qc, kc, vc)


def _dsa_kernel(qd_ref, qi_ref, wi_ref, kdki_ref, vd_ref, o_ref,
                key_sc, cnt_sc, tie_sc, m_sc, l_sc, acc_sc, *, t, topk):
    i = pl.program_id(1)
    ntile = i + 1
    qi = qi_ref[0]
    wi = wi_ref[0]

    def key_rows(jt):
        return kdki_ref[0, pl.ds(pl.multiple_of(jt * t, t), t), :]

    def score_keys(jt):
        ki = key_rows(jt)[:, DSA_HD:DSA_HD + IDX_HD]
        acc = jnp.zeros((t, t), F32)
        for h in range(IDX_HEADS):
            lg = _nt(qi[:, h * IDX_HD:(h + 1) * IDX_HD], ki)
            acc = acc + jnp.maximum(lg, 0.0) * wi[:, h:h + 1]
        bits = lax.bitcast_convert_type(acc, jnp.int32)
        key = jnp.where(bits < 0, bits ^ jnp.int32(0x7FFFFFFF), bits)
        return jnp.where(acc == 0.0, 0, key)

    def p1(jt, carry):
        key_sc[jt] = score_keys(jt)
        return carry
    lax.fori_loop(0, i, p1, 0)
    row = lax.broadcasted_iota(jnp.int32, (t, t), 0)
    col = lax.broadcasted_iota(jnp.int32, (t, t), 1)
    key_sc[i] = jnp.where(col <= row, score_keys(i), INT_MIN)

    def count_ge(cand):
        cnt_sc[...] = jnp.zeros(cnt_sc.shape, jnp.int32)

        def body(jt, carry):
            ge = jnp.where(key_sc[jt] >= cand, 1, 0)
            part = ge[:, 0:128]
            for c in range(1, t // 128):
                part = part + ge[:, c * 128:(c + 1) * 128]
            cnt_sc[...] = cnt_sc[...] + part
            return carry
        lax.fori_loop(0, ntile, body, 0)
        return jnp.sum(cnt_sc[...], axis=1, keepdims=True)

    def bis(it, thr):
        bit = lax.shift_left(jnp.int32(1), jnp.int32(31) - it)
        cand = thr ^ bit
        return jnp.where(count_ge(cand) >= topk, cand, thr)
    thr = lax.fori_loop(0, 32, bis, jnp.full((t, 1), INT_MIN, jnp.int32))
    thr = jnp.maximum(thr, INT_MIN + 1)
    n_ge = count_ge(thr)

    @pl.when(jnp.max(n_ge) > topk)
    def _():
        n_gt = count_ge(thr + 1)
        quota = (topk - n_gt).astype(F32)
        tie_sc[...] = jnp.zeros(tie_sc.shape, F32)
        lower = jnp.where(row < col, 1.0, 0.0).astype(BF16)

        def fix(jt, carry):
            key = key_sc[jt]
            eq = key == thr
            eqf = jnp.where(eq, 1.0, 0.0)
            rank = _nn(eqf.astype(BF16), lower) + tie_sc[...]
            drop = jnp.logical_and(eq, rank >= quota)
            key_sc[jt] = jnp.where(drop, INT_MIN, key)
            tie_sc[...] = tie_sc[...] + jnp.sum(eqf, axis=1, keepdims=True)
            return carry
        lax.fori_loop(0, ntile, fix, 0)

    m_sc[...] = jnp.full(m_sc.shape, NEG_BIG, F32)
    l_sc[...] = jnp.zeros(l_sc.shape, F32)
    acc_sc[...] = jnp.zeros(acc_sc.shape, F32)
    qd = qd_ref[0]

    def p3(jt, carry):
        sel = key_sc[jt] >= thr
        kd = key_rows(jt)[:, 0:DSA_HD]
        vd = vd_ref[0, pl.ds(pl.multiple_of(jt * t, t), t), :]
        for h in range(DSA_HEADS):
            s = jnp.where(sel, _nt(qd[:, h * DSA_HD:(h + 1) * DSA_HD], kd), NEG_BIG)
            m_prev = m_sc[h]
            m_cur = jnp.maximum(m_prev, jnp.max(s, axis=1, keepdims=True))
            alpha = jnp.exp(m_prev - m_cur)
            p = jnp.exp(s - m_cur)
            l_sc[h] = alpha * l_sc[h] + jnp.sum(p, axis=1, keepdims=True)
            acc_sc[h] = alpha * acc_sc[h] + _nn(p.astype(BF16), vd)
            m_sc[h] = m_cur
        return carry
    lax.fori_loop(0, ntile, p3, 0)
    o_ref[0] = jnp.concatenate(
        [acc_sc[h][:, 0:DSA_HD] / l_sc[h] for h in range(DSA_HEADS)], axis=1).astype(o_ref.dtype)


def dsa_attention(qd, qi, wi, kdki, vd, t=512):
    b, s, w = qd.shape
    t = min(t, s)
    n = s // t
    topk = min(TOPK_MAX, s // 4)
    qspec = pl.BlockSpec((1, t, w), lambda bi, i: (bi, i, 0))
    kspec = pl.BlockSpec((1, s, 128), lambda bi, i: (bi, 0, 0))
    return pl.pallas_call(
        functools.partial(_dsa_kernel, t=t, topk=topk),
        grid=(b, n),
        in_specs=[qspec, qspec, pl.BlockSpec((1, t, IDX_HEADS), lambda bi, i: (bi, i, 0)), kspec, kspec],
        out_specs=pl.BlockSpec((1, t, w), lambda bi, i: (bi, i, 0)),
        out_shape=jax.ShapeDtypeStruct((b, s, w), BF16),
        scratch_shapes=[pltpu.VMEM((n, t, t), jnp.int32), pltpu.VMEM((t, 128), jnp.int32),
                        pltpu.VMEM((t, 1), F32),
                        pltpu.VMEM((DSA_HEADS, t, 1), F32), pltpu.VMEM((DSA_HEADS, t, 1), F32),
                        pltpu.VMEM((DSA_HEADS, t, 128), F32)],
        compiler_params=_cparams(("arbitrary", "arbitrary")),
        name="dsa_attention",
    )(qd, qi, wi, kdki, vd)


def _merge_kernel(oa_ref, ob_ref, oc_ref, od_ref, gt_ref, wb_ref, wo_ref, x_ref, mod_ref, pg_ref, o_ref,
                  *, gate_row):
    d = x_ref.shape[2]
    merged = None
    for bi_, o_r in enumerate((oa_ref, ob_ref, oc_ref, od_ref)):
        proj = _nn(o_r[0], wb_ref[bi_])
        gate = jax.nn.sigmoid(gt_ref[0, :, bi_ * d:(bi_ + 1) * d].astype(F32))
        merged = gate * proj if merged is None else merged + gate * proj
    y = _nn(merged.astype(BF16), wo_ref[...])
    yn = y * lax.rsqrt(jnp.mean(y * y, axis=-1, keepdims=True) + EPS) * pg_ref[...]
    o_ref[0] = x_ref[0] + mod_ref[0][gate_row:gate_row + 1] * yn


def merge_out(oa, ob, oc, od, gates, w_branch, w_out, x, mod6, post_g, tm=512):
    b, s, d = x.shape
    tm = min(tm, s)
    ospec = pl.BlockSpec((1, tm, BRANCH_W), lambda bi, i: (bi, i, 0))
    return pl.pallas_call(
        functools.partial(_merge_kernel, gate_row=2),
        grid=(b, s // tm),
        in_specs=[ospec, ospec, ospec, ospec,
                  pl.BlockSpec((1, tm, N_BRANCH * d), lambda bi, i: (bi, i, 0)),
                  pl.BlockSpec((N_BRANCH, BRANCH_W, d), lambda bi, i: (0, 0, 0)),
                  pl.BlockSpec((d, d), lambda bi, i: (0, 0)),
                  pl.BlockSpec((1, tm, d), lambda bi, i: (bi, i, 0)),
                  pl.BlockSpec((1, 6, d), lambda bi, i: (bi, 0, 0)),
                  pl.BlockSpec((1, d), lambda bi, i: (0, 0))],
        out_specs=pl.BlockSpec((1, tm, d), lambda bi, i: (bi, i, 0)),
        out_shape=jax.ShapeDtypeStruct((b, s, d), F32),
        compiler_params=_cparams(("arbitrary", "arbitrary")),
        name="merge_out",
    )(oa, ob, oc, od, gates, w_branch, w_out, x, mod6, post_g.reshape(1, d))


FFN_HALO = 16


def _ffn_kernel(x_ref, halo_ref, mod_ref, pre_g_ref, wg_ref, wv_ref, cwg_ref, cwv_ref, cbg_ref, cbv_ref,
                wd_ref, post_g_ref, o_ref, h_sc, acc_sc, *, tm):
    i = pl.program_id(1)
    c = pl.program_id(2)
    nc = pl.num_programs(2)
    mod = mod_ref[0]

    def modulated(xv):
        ms = jnp.mean(xv * xv, axis=-1, keepdims=True)
        y = xv * lax.rsqrt(ms + EPS) * pre_g_ref[...]
        return y * (1.0 + mod[4:5]) + mod[3:4]

    @pl.when(c == 0)
    def _():
        hh = modulated(halo_ref[0])
        h_sc[0:FFN_HALO, :] = jnp.where(i == 0, 0.0, hh).astype(BF16)
        h_sc[FFN_HALO:FFN_HALO + tm, :] = modulated(x_ref[0]).astype(BF16)
        acc_sc[...] = jnp.zeros(acc_sc.shape, F32)

    hext = h_sc[...]

    def conv(u, cw_ref, cb_ref):
        cw = cw_ref[...]
        out = cb_ref[...] + cw[2:3] * u[FFN_HALO:FFN_HALO + tm]
        out = out + cw[1:2] * u[FFN_HALO - 1:FFN_HALO - 1 + tm]
        return out + cw[0:1] * u[FFN_HALO - 2:FFN_HALO - 2 + tm]

    gate = conv(_nn(hext, wg_ref[...]), cwg_ref, cbg_ref)
    val = conv(_nn(hext, wv_ref[...]), cwv_ref, cbv_ref)
    act = (gate * jax.nn.sigmoid(gate) * val).astype(BF16)
    acc_sc[...] = acc_sc[...] + _nn(act, wd_ref[...])

    @pl.when(c == nc - 1)
    def _():
        y = acc_sc[...]
        yn = y * lax.rsqrt(jnp.mean(y * y, axis=-1, keepdims=True) + EPS) * post_g_ref[...]
        o_ref[0] = x_ref[0] + mod[5:6] * yn


def conv_ffn(x, mod6, pre_g, w_up, conv_w, conv_b, w_down, post_g, tm=512, cf=256):
    b, s, d = x.shape
    dff = w_down.shape[0]
    tm = min(tm, s)
    nc = dff // cf
    r = tm // FFN_HALO
    cb2 = conv_b.reshape(1, 2 * dff)
    return pl.pallas_call(
        functools.partial(_ffn_kernel, tm=tm),
        grid=(b, s // tm, nc),
        in_specs=[pl.BlockSpec((1, tm, d), lambda bi, i, c: (bi, i, 0)),
                  pl.BlockSpec((1, FFN_HALO, d), lambda bi, i, c: (bi, jnp.maximum(i * r - 1, 0), 0)),
                  pl.BlockSpec((1, 6, d), lambda bi, i, c: (bi, 0, 0)),
                  pl.BlockSpec((1, d), lambda bi, i, c: (0, 0)),
                  pl.BlockSpec((d, cf), lambda bi, i, c: (0, c)),
                  pl.BlockSpec((d, cf), lambda bi, i, c: (0, nc + c)),
                  pl.BlockSpec((FFN_CONV_K, cf), lambda bi, i, c: (0, c)),
                  pl.BlockSpec((FFN_CONV_K, cf), lambda bi, i, c: (0, nc + c)),
                  pl.BlockSpec((1, cf), lambda bi, i, c: (0, c)),
                  pl.BlockSpec((1, cf), lambda bi, i, c: (0, nc + c)),
                  pl.BlockSpec((cf, d), lambda bi, i, c: (c, 0)),
                  pl.BlockSpec((1, d), lambda bi, i, c: (0, 0))],
        out_specs=pl.BlockSpec((1, tm, d), lambda bi, i, c: (bi, i, 0)),
        out_shape=jax.ShapeDtypeStruct((b, s, d), F32),
        scratch_shapes=[pltpu.VMEM((tm + FFN_HALO, d), BF16), pltpu.VMEM((tm, d), F32)],
        compiler_params=_cparams(("arbitrary", "arbitrary", "arbitrary")),
        name="conv_ffn",
    )(x, x, mod6, pre_g.reshape(1, d), w_up, w_up, conv_w, conv_w, cb2, cb2, w_down, post_g.reshape(1, d))


def _reorder_w_in(w_in_l):
    sizes = (512, 256, 256, 256, 256, 256, 256, 256, 64, 64, 256, 32, 8, GATES_W)
    offs = [0]
    for sz in sizes:
        offs.append(offs[-1] + sz)
    col = lambda idx: w_in_l[:, offs[idx]:offs[idx + 1]]
    d = w_in_l.shape[0]
    misc = jnp.concatenate([col(8), col(9), col(11), col(12), jnp.zeros((d, 256 - 168), w_in_l.dtype)], axis=1)
    main = jnp.concatenate([col(0), col(1), col(2), col(3), col(4), col(5), col(6), col(7), misc, col(10)],
                           axis=1)
    return main.astype(BF16), col(13).astype(BF16)


def kernel(x, c, positions, ada_w, ada_b, mix_pre_g, mix_post_g, ffn_pre_g, ffn_post_g, w_in, conv_a_w,
           conv_a_b, conv_a_ln_g, conv_a_ln_b, lam_q1, lam_k1, lam_q2, lam_k2, diff_subln_g, w_branch,
           w_out, w_up, ffn_conv_w, ffn_conv_b, w_down):
    depth = ada_w.shape[0]
    b, s, d = x.shape
    mod_all = adaln_mod(c, ada_w, ada_b)
    tabs = rope_tables(positions)
    for l in range(depth):
        lam_init = 0.8 - 0.6 * math.exp(-0.3 * l)
        mod6 = mod_all[l].reshape(b, 6, d)
        w_main, w_gates = _reorder_w_in(w_in[l])
        zmain = norm_mod_matmul(x, mod6, mix_pre_g[l], w_main, F32, 0, 1, name="in_proj_main")
        gates = norm_mod_matmul(x, mod6, mix_pre_g[l], w_gates, BF16, 0, 1, name="in_proj_gates")
        qb, kb, vb, qc, kc, vc, qd, kdki, vd, qi, wi = prep(zmain, tabs)
        oa = conformer(zmain, conv_a_w[l], conv_a_b[l], conv_a_ln_g[l], conv_a_ln_b[l])
        ob = diff_attention(qb, kb, vb, lam_q1[l], lam_k1[l], lam_q2[l], lam_k2[l], diff_subln_g[l], lam_init)
        oc = sb_attention(qc, kc, vc)
        od = dsa_attention(qd, qi, wi, kdki, vd)
        x = merge_out(oa, ob, oc, od, gates, w_branch[l].astype(BF16), w_out[l].astype(BF16), x, mod6,
                      mix_post_g[l])
        x = conv_ffn(x, mod6, ffn_pre_g[l], w_up[l].astype(BF16), ffn_conv_w[l], ffn_conv_b[l],
                     w_down[l].astype(BF16), ffn_post_g[l])
    return x
```

```python
import functools
import math

import jax
import jax.numpy as jnp
from jax import lax
from jax.experimental import pallas as pl
from jax.experimental.pallas import tpu as pltpu

F32 = jnp.float32
BF16 = jnp.bfloat16

N_BRANCH = 4
BRANCH_W = 256
ROPE_THETA = 10000.0
EPS = 1e-6
CONV_CH = 256
CONV_K = 31
DIFF_VD = 64
DIFF_QK = 32
DIFF_HEADS = 4
SB_HD = 64
SB_HEADS = 4
DSA_HD = 64
DSA_HEADS = 4
IDX_HEADS = 8
IDX_HD = 32
TOPK_MAX = 256
FFN_CONV_K = 3

ZMAIN_W = 2816
GATES_W = N_BRANCH * 1024
MISC_KD, MISC_VD, MISC_KI, MISC_WI = 0, 64, 128, 160

VMEM_LIMIT = 56 * 1024 * 1024
INT_MIN = -(2 ** 31)
NEG_BIG = -1e30
LOG2E = math.log2(math.e)
ATT_T = 512


def _cparams(sem):
    return pltpu.CompilerParams(dimension_semantics=sem, vmem_limit_bytes=VMEM_LIMIT)


def _nn(a, b):
    return jnp.dot(a, b, preferred_element_type=F32)


def _mod_kernel(c_ref, w_ref, b_ref, o_ref):
    c = c_ref[...]
    c_act = c * jax.nn.sigmoid(c)
    o_ref[0] = _nn(c_act, w_ref[0]) + b_ref[0]


def adaln_mod(c, ada_w, ada_b):
    depth, d, n = ada_w.shape
    b = c.shape[0]
    tn = 1024
    return pl.pallas_call(
        _mod_kernel,
        grid=(depth, n // tn),
        in_specs=[pl.BlockSpec((b, d), lambda l, j: (0, 0)),
                  pl.BlockSpec((1, d, tn), lambda l, j: (l, 0, j)),
                  pl.BlockSpec((1, 1, tn), lambda l, j: (l, 0, j))],
        out_specs=pl.BlockSpec((1, b, tn), lambda l, j: (l, 0, j)),
        out_shape=jax.ShapeDtypeStruct((depth, b, n), F32),
        compiler_params=_cparams(("arbitrary", "arbitrary")),
        name="adaln_mod",
    )(c, ada_w, ada_b.reshape(depth, 1, n))


def _rope_table_kernel(pos_ref, f_ref, c32_ref, s32_ref, c64_ref, s64_ref):
    pos = pos_ref[0].astype(F32)
    f = f_ref[...]
    a32 = pos * f[0:1]
    a64 = pos * f[2:3]
    c32_ref[0] = jnp.cos(a32)
    s32_ref[0] = jnp.sin(a32) * f[1:2]
    c64_ref[0] = jnp.cos(a64)
    s64_ref[0] = jnp.sin(a64) * f[3:4]


def rope_tables(positions, ts=512):
    b, s = positions.shape
    ts = min(ts, s)

    def pattern(d):
        inv = ROPE_THETA ** (-jnp.arange(0, d, 2, dtype=F32) / d)
        freq = jnp.tile(jnp.concatenate([inv, inv]), 128 // d)
        sign = jnp.tile(jnp.concatenate([-jnp.ones(d // 2, F32), jnp.ones(d // 2, F32)]), 128 // d)
        return freq, sign

    f32_, s32_ = pattern(IDX_HD)
    f64_, s64_ = pattern(DSA_HD)
    ftab = jnp.stack([f32_, s32_, f64_, s64_])
    tab = jax.ShapeDtypeStruct((b, s, 128), F32)
    spec = pl.BlockSpec((1, ts, 128), lambda bi, i: (bi, i, 0))
    return pl.pallas_call(
        _rope_table_kernel,
        grid=(b, s // ts),
        in_specs=[pl.BlockSpec((1, ts, 1), lambda bi, i: (bi, i, 0)),
                  pl.BlockSpec((4, 128), lambda bi, i: (0, 0))],
        out_specs=[spec] * 4,
        out_shape=[tab] * 4,
        compiler_params=_cparams(("arbitrary", "arbitrary")),
        name="rope_tables",
    )(positions.reshape(b, s, 1), ftab)


def _nmm_kernel(x_ref, mod_ref, g_ref, w_ref, o_ref, *, shift_row, scale_row):
    x = x_ref[0]
    mod = mod_ref[0]
    ms = jnp.mean(x * x, axis=-1, keepdims=True)
    y = x * lax.rsqrt(ms + EPS) * g_ref[...]
    h = y * (1.0 + mod[scale_row:scale_row + 1]) + mod[shift_row:shift_row + 1]
    o_ref[0] = _nn(h.astype(BF16), w_ref[...]).astype(o_ref.dtype)


def norm_mod_matmul(x, mod6, g, w, out_dtype, shift_row, scale_row, tm=512, name="nmm"):
    b, s, d = x.shape
    n = w.shape[1]
    tm = min(tm, s)
    return pl.pallas_call(
        functools.partial(_nmm_kernel, shift_row=shift_row, scale_row=scale_row),
        grid=(b, s // tm),
        in_specs=[pl.BlockSpec((1, tm, d), lambda bi, i: (bi, i, 0)),
                  pl.BlockSpec((1, 6, d), lambda bi, i: (bi, 0, 0)),
                  pl.BlockSpec((1, d), lambda bi, i: (0, 0)),
                  pl.BlockSpec((d, n), lambda bi, i: (0, 0))],
        out_specs=pl.BlockSpec((1, tm, n), lambda bi, i: (bi, i, 0)),
        out_shape=jax.ShapeDtypeStruct((b, s, n), out_dtype),
        compiler_params=_cparams(("arbitrary", "arbitrary")),
        name=name,
    )(x, mod6, g.reshape(1, d), w)


def _rope(x, cos, sin_signed, half):
    w = x.shape[1]
    lane = lax.broadcasted_iota(jnp.int32, x.shape, 1)
    first = (lane % (2 * half)) < half
    swapped = jnp.where(first, pltpu.roll(x, w - half, 1), pltpu.roll(x, half, 1))
    return x * cos + swapped * sin_signed


def _prep_kernel(qb_ref, kb_ref, vb_ref, qc_ref, kc_ref, vc_ref, qd_ref, md_ref, qi_ref,
                 c32_ref, s32_ref, c64_ref, s64_ref,
                 oqbT, okb, ovbT, oqcT, okc, ovcT, oqdT, oqiT, okdki, ovdT, owiT):
    c32 = jnp.concatenate([c32_ref[0]] * 2, axis=1)
    s32 = jnp.concatenate([s32_ref[0]] * 2, axis=1)
    c64 = jnp.concatenate([c64_ref[0]] * 2, axis=1)
    s64 = jnp.concatenate([s64_ref[0]] * 2, axis=1)
    h32, h64 = IDX_HD // 2, DSA_HD // 2
    oqbT[0] = (_rope(qb_ref[0], c32, s32, h32) * (DIFF_QK ** -0.5 * LOG2E)).T.astype(BF16)
    okb[0] = _rope(kb_ref[0], c32, s32, h32).astype(BF16)
    ovbT[0] = vb_ref[0].T.astype(BF16)
    oqcT[0] = (qc_ref[0] * (SB_HD ** -0.5)).T.astype(BF16)
    okc[0] = kc_ref[0].astype(BF16)
    ovcT[0] = vc_ref[0].T.astype(BF16)
    oqdT[0] = (_rope(qd_ref[0], c64, s64, h64) * (DSA_HD ** -0.5 * LOG2E)).T.astype(BF16)
    oqiT[0] = _rope(qi_ref[0], c32, s32, h32).T.astype(BF16)
    md = md_ref[0]
    md64 = _rope(md, c64, s64, h64)
    md32 = _rope(md, c32, s32, h32)
    ts = md.shape[0]
    okdki[0] = jnp.concatenate(
        [md64[:, MISC_KD:MISC_KD + DSA_HD], md32[:, MISC_KI:MISC_KI + IDX_HD],
         jnp.zeros((ts, 128 - DSA_HD - IDX_HD), F32)], axis=1).astype(BF16)
    mdT = md.T
    ovdT[0, 0] = mdT[MISC_VD:MISC_VD + DSA_HD].astype(BF16)
    owiT[0] = mdT[MISC_WI:MISC_WI + IDX_HEADS] * (IDX_HEADS ** -0.5 * IDX_HD ** -0.5)


def prep(zmain, tabs, ts):
    b, s, _ = zmain.shape
    zspec = lambda cb: pl.BlockSpec((1, ts, 256), lambda bi, i, cb=cb: (bi, i, cb))
    tspec = pl.BlockSpec((1, ts, 128), lambda bi, i: (bi, i, 0))
    row = pl.BlockSpec((1, ts, 256), lambda bi, i: (bi, i, 0))
    colT = pl.BlockSpec((1, 256, ts), lambda bi, i: (bi, 0, i))
    s_row = jax.ShapeDtypeStruct((b, s, 256), BF16)
    s_colT = jax.ShapeDtypeStruct((b, 256, s), BF16)
    out_specs = [colT, row, colT, colT, row, colT, colT, colT,
                 pl.BlockSpec((1, ts, 128), lambda bi, i: (bi, i, 0)),
                 pl.BlockSpec((1, 1, DSA_HD, ts), lambda bi, i: (bi, i, 0, 0)),
                 pl.BlockSpec((1, IDX_HEADS, ts), lambda bi, i: (bi, 0, i))]
    out_shape = [s_colT, s_row, s_colT, s_colT, s_row, s_colT, s_colT, s_colT,
                 jax.ShapeDtypeStruct((b, s, 128), BF16),
                 jax.ShapeDtypeStruct((b, s // ts, DSA_HD, ts), BF16),
                 jax.ShapeDtypeStruct((b, IDX_HEADS, s), F32)]
    return pl.pallas_call(
        _prep_kernel,
        grid=(b, s // ts),
        in_specs=[zspec(cb) for cb in range(2, 11)] + [tspec] * 4,
        out_specs=out_specs,
        out_shape=out_shape,
        compiler_params=_cparams(("arbitrary", "arbitrary")),
        name="prep",
    )(*([zmain] * 9), *tabs)


def _masked_rows(xT, lo, hi):
    r = lax.broadcasted_iota(jnp.int32, xT.shape, 0)
    return jnp.where(jnp.logical_and(r >= lo, r < hi), xT, jnp.zeros_like(xT))


CONV_HALO = 32


def _conformer_kernel(u_ref, halo_ref, w_ref, b_ref, g_ref, beta_ref, o_ref, buf, *, ts):
    i = pl.program_id(1)
    u = u_ref[0]
    buf[CONV_HALO:CONV_HALO + ts, :] = u[:, :CONV_CH] * jax.nn.sigmoid(u[:, CONV_CH:])
    uh = halo_ref[0]
    hh = uh[:, :CONV_CH] * jax.nn.sigmoid(uh[:, CONV_CH:])
    buf[0:CONV_HALO, :] = jnp.where(i == 0, 0.0, hh)
    w = w_ref[...]
    acc = jnp.zeros((ts, CONV_CH), F32) + b_ref[...]
    for k in range(CONV_K):
        acc = acc + w[k:k + 1] * buf[pl.ds(CONV_HALO - (CONV_K - 1) + k, ts), :]
    mu = jnp.mean(acc, axis=-1, keepdims=True)
    xc = acc - mu
    y = xc * lax.rsqrt(jnp.mean(xc * xc, axis=-1, keepdims=True) + EPS) * g_ref[...] + beta_ref[...]
    o_ref[0] = (y * jax.nn.sigmoid(y)).astype(o_ref.dtype)


def conformer(zmain, conv_w, conv_b, ln_g, ln_b, ts=512):
    b, s, _ = zmain.shape
    ts = min(ts, s)
    r = ts // CONV_HALO
    wpad = jnp.concatenate([conv_w, jnp.zeros((32 - CONV_K, CONV_CH), F32)], axis=0)
    vec = lambda a: a.reshape(1, CONV_CH)
    vspec = pl.BlockSpec((1, CONV_CH), lambda bi, i: (0, 0))
    return pl.pallas_call(
        functools.partial(_conformer_kernel, ts=ts),
        grid=(b, s // ts),
        in_specs=[pl.BlockSpec((1, ts, 2 * CONV_CH), lambda bi, i: (bi, i, 0)),
                  pl.BlockSpec((1, CONV_HALO, 2 * CONV_CH), lambda bi, i: (bi, jnp.maximum(i * r - 1, 0), 0)),
                  pl.BlockSpec((32, CONV_CH), lambda bi, i: (0, 0)),
                  vspec, vspec, vspec],
        out_specs=pl.BlockSpec((1, ts, CONV_CH), lambda bi, i: (bi, i, 0)),
        out_shape=jax.ShapeDtypeStruct((b, s, CONV_CH), BF16),
        scratch_shapes=[pltpu.VMEM((ts + CONV_HALO, CONV_CH), F32)],
        compiler_params=_cparams(("arbitrary", "arbitrary")),
        name="conformer",
    )(zmain, zmain, wpad, vec(conv_b), vec(ln_g), vec(ln_b))


def _diff_kernel(qT_ref, k_ref, vT_ref, lq1, lk1, lq2, lk2, sg_ref, o_ref, qz_sc, m_sc, l_sc, acc_sc,
                 *, t, lam_init):
    i = pl.program_id(1)
    j = pl.program_id(2)
    nmaps = 2 * DIFF_HEADS

    @pl.when(j == 0)
    def _():
        m_sc[...] = jnp.full(m_sc.shape, -jnp.inf, F32)
        l_sc[...] = jnp.zeros(l_sc.shape, F32)
        acc_sc[...] = jnp.zeros(acc_sc.shape, F32)
        qT = qT_ref[0].astype(F32)
        for jj in range(nmaps):
            g, r = divmod(jj * DIFF_QK, 128)
            qz_sc[jj] = _masked_rows(qT[g * 128:(g + 1) * 128], r, r + DIFF_QK).astype(BF16)

    def step(masked):
        k = k_ref[0]
        vT = vT_ref[0]
        if masked:
            key_i = lax.broadcasted_iota(jnp.int32, (t, t), 0)
            qry_i = lax.broadcasted_iota(jnp.int32, (t, t), 1)
            causal = key_i <= qry_i
        for jj in range(nmaps):
            g = (jj * DIFF_QK) // 128
            h = jj // 2
            s = _nn(k[:, g * 128:(g + 1) * 128], qz_sc[jj])
            if masked:
                s = jnp.where(causal, s, -jnp.inf)
            m_prev = m_sc[jj]
            m_cur = jnp.maximum(m_prev, jnp.max(s, axis=0, keepdims=True))
            alpha = jnp.exp2(m_prev - m_cur)
            p = jnp.exp2(s - m_cur)
            l_sc[jj] = alpha * l_sc[jj] + jnp.sum(p, axis=0, keepdims=True)
            acc_sc[jj] = alpha * acc_sc[jj] + _nn(vT[h * DIFF_VD:(h + 1) * DIFF_VD], p.astype(BF16))
            m_sc[jj] = m_cur

    @pl.when(j < i)
    def _():
        step(False)

    @pl.when(j == i)
    def _():
        step(True)
        lam = (jnp.exp(jnp.sum(lq1[...].astype(F32) * lk1[...].astype(F32), axis=1, keepdims=True))
               - jnp.exp(jnp.sum(lq2[...].astype(F32) * lk2[...].astype(F32), axis=1, keepdims=True))
               + lam_init)
        outs = []
        for h in range(DIFF_HEADS):
            o0 = acc_sc[2 * h] / l_sc[2 * h]
            o1 = acc_sc[2 * h + 1] / l_sc[2 * h + 1]
            o = o0 - lam * o1
            o = o * lax.rsqrt(jnp.mean(o * o, axis=0, keepdims=True) + EPS) * sg_ref[...]
            outs.append(o * (1.0 - lam_init))
        o_ref[0] = jnp.concatenate(outs, axis=0).T.astype(o_ref.dtype)


def diff_attention(qbT, kb, vbT, lq1, lk1, lq2, lk2, subln_g, lam_init, t):
    b, s, w = kb.shape
    n = s // t
    nmaps = 2 * DIFF_HEADS
    qspec = pl.BlockSpec((1, w, t), lambda bi, i, j: (bi, 0, i))
    kspec = pl.BlockSpec((1, t, w), lambda bi, i, j: (bi, jnp.minimum(j, i), 0))
    vspec = pl.BlockSpec((1, w, t), lambda bi, i, j: (bi, 0, jnp.minimum(j, i)))
    pspec = pl.BlockSpec((1, DIFF_QK), lambda bi, i, j: (0, 0))
    return pl.pallas_call(
        functools.partial(_diff_kernel, t=t, lam_init=lam_init),
        grid=(b, n, n),
        in_specs=[qspec, kspec, vspec, pspec, pspec, pspec, pspec,
                  pl.BlockSpec((DIFF_VD, 1), lambda bi, i, j: (0, 0))],
        out_specs=pl.BlockSpec((1, t, w), lambda bi, i, j: (bi, i, 0)),
        out_shape=jax.ShapeDtypeStruct((b, s, w), BF16),
        scratch_shapes=[pltpu.VMEM((nmaps, 128, t), BF16),
                        pltpu.VMEM((nmaps, 1, t), F32), pltpu.VMEM((nmaps, 1, t), F32),
                        pltpu.VMEM((nmaps, DIFF_VD, t), F32)],
        compiler_params=_cparams(("arbitrary", "arbitrary", "arbitrary")),
        name="diff_attention",
    )(qbT, kb, vbT, lq1.reshape(1, -1), lk1.reshape(1, -1), lq2.reshape(1, -1), lk2.reshape(1, -1),
      subln_g.reshape(-1, 1))


SB_BLK = 128


def _sb_kernel(qT_ref, k_ref, vT_ref, o_ref, qz_sc, c_sc, acc_sc, *, t):
    i = pl.program_id(1)
    j = pl.program_id(2)

    @pl.when(j == 0)
    def _():
        c_sc[...] = jnp.zeros(c_sc.shape, F32)
        acc_sc[...] = jnp.zeros(acc_sc.shape, F32)
        qT = qT_ref[0].astype(F32)
        for h in range(SB_HEADS):
            g, r = divmod(h * SB_HD, 128)
            qz_sc[h] = _masked_rows(qT[g * 128:(g + 1) * 128], r, r + SB_HD).astype(BF16)

    def step(masked):
        k = k_ref[0]
        vT = vT_ref[0]
        ra = lax.broadcasted_iota(jnp.int32, (SB_BLK, SB_BLK), 0)
        rb = lax.broadcasted_iota(jnp.int32, (SB_BLK, SB_BLK), 1)
        upper = jnp.where(rb > ra, 1.0, 0.0).astype(BF16)
        if masked:
            key_i = lax.broadcasted_iota(jnp.int32, (t, t), 0)
            qry_i = lax.broadcasted_iota(jnp.int32, (t, t), 1)
            strict = key_i < qry_i
        for h in range(SB_HEADS):
            g = (h * SB_HD) // 128
            z = _nn(k[:, g * 128:(g + 1) * 128], qz_sc[h])
            neg_abs = lax.bitcast_convert_type(
                lax.bitcast_convert_type(z, jnp.uint32) | jnp.uint32(0x80000000), F32)
            ls_pos = jnp.minimum(z, 0.0) - jnp.log(1.0 + jnp.exp(neg_abs))
            keep = ls_pos - z
            if masked:
                keep = jnp.where(strict, keep, 0.0)
            hi = keep.astype(BF16)
            lo = (keep - hi.astype(F32)).astype(BF16)
            carry = c_sc[h]
            parts = []
            for blk in range(t // SB_BLK - 1, -1, -1):
                sl = slice(blk * SB_BLK, (blk + 1) * SB_BLK)
                parts.append(_nn(upper, hi[sl]) + _nn(upper, lo[sl]) + carry)
                carry = carry + jnp.sum(keep[sl], axis=0, keepdims=True)
            c_sc[h] = carry
            later = jnp.concatenate(parts[::-1], axis=0)
            wgt = jnp.exp(ls_pos + later)
            if masked:
                wgt = jnp.where(strict, wgt, 0.0)
            acc_sc[h] = acc_sc[h] + _nn(vT[h * SB_HD:(h + 1) * SB_HD], wgt.astype(BF16))

    @pl.when(j == 0)
    def _():
        step(True)

    @pl.when(jnp.logical_and(j > 0, j <= i))
    def _():
        step(False)

    @pl.when(j == i)
    def _():
        o_ref[0] = jnp.concatenate([acc_sc[h] for h in range(SB_HEADS)], axis=0).T.astype(o_ref.dtype)


def sb_attention(qcT, kc, vcT, t):
    b, s, w = kc.shape
    n = s // t
    qspec = pl.BlockSpec((1, w, t), lambda bi, i, j: (bi, 0, i))
    kspec = pl.BlockSpec((1, t, w), lambda bi, i, j: (bi, jnp.maximum(i - j, 0), 0))
    vspec = pl.BlockSpec((1, w, t), lambda bi, i, j: (bi, 0, jnp.maximum(i - j, 0)))
    return pl.pallas_call(
        functools.partial(_sb_kernel, t=t),
        grid=(b, n, n),
        in_specs=[qspec, kspec, vspec],
        out_specs=pl.BlockSpec((1, t, w), lambda bi, i, j: (bi, i, 0)),
        out_shape=jax.ShapeDtypeStruct((b, s, w), BF16),
        scratch_shapes=[pltpu.VMEM((SB_HEADS, 128, t), BF16),
                        pltpu.VMEM((SB_HEADS, 1, t), F32), pltpu.VMEM((SB_HEADS, SB_HD, t), F32)],
        compiler_params=_cparams(("arbitrary", "arbitrary", "arbitrary")),
        name="sb_attention",
    )(qcT, kc, vcT)


def _dsa_kernel(qdT_ref, qiT_ref, wiT_ref, kdki_ref, vdT_ref, o_ref,
                key_sc, qz_sc, qiz_sc, tie_sc, m_sc, l_sc, acc_sc, *, t, topk):
    i = pl.program_id(1)
    ntile = i + 1
    wiT = wiT_ref[0]

    qdT = qdT_ref[0].astype(F32)
    zeros64 = jnp.zeros((128 - DSA_HD, t), F32)
    for h in range(DSA_HEADS):
        qz_sc[h] = jnp.concatenate([qdT[h * DSA_HD:(h + 1) * DSA_HD], zeros64], axis=0).astype(BF16)
    qiT = qiT_ref[0].astype(F32)
    for h in range(IDX_HEADS):
        qiz_sc[h] = jnp.concatenate(
            [jnp.zeros((DSA_HD, t), F32), qiT[h * IDX_HD:(h + 1) * IDX_HD],
             jnp.zeros((128 - DSA_HD - IDX_HD, t), F32)], axis=0).astype(BF16)

    def key_rows(jt):
        return kdki_ref[0, pl.ds(pl.multiple_of(jt * t, t), t), :]

    def score_keys(jt):
        kk = key_rows(jt)
        acc = jnp.zeros((t, t), F32)
        for h in range(IDX_HEADS):
            acc = acc + jnp.maximum(_nn(kk, qiz_sc[h]), 0.0) * wiT[h:h + 1]
        bits = lax.bitcast_convert_type(acc, jnp.int32)
        key = jnp.where(bits < 0, bits ^ jnp.int32(0x7FFFFFFF), bits)
        return jnp.where(acc == 0.0, 0, key)

    def p1(jt, carry):
        key_sc[jt] = score_keys(jt)
        return carry
    lax.fori_loop(0, i, p1, 0)
    key_i = lax.broadcasted_iota(jnp.int32, (t, t), 0)
    qry_i = lax.broadcasted_iota(jnp.int32, (t, t), 1)
    key_sc[i] = jnp.where(key_i <= qry_i, score_keys(i), INT_MIN)

    def count_ge(cand):
        def body(jt, part):
            ge = jnp.where(key_sc[jt] >= cand, 1, 0)
            return part + jnp.sum(ge.reshape(t // 8, 8, t), axis=0)
        part = lax.fori_loop(0, ntile, body, jnp.zeros((8, t), jnp.int32))
        return jnp.sum(part, axis=0, keepdims=True)

    def bis(it, thr):
        bit = lax.shift_left(jnp.int32(1), jnp.int32(31) - it)
        cand = thr ^ bit
        return jnp.where(count_ge(cand) >= topk, cand, thr)
    thr = lax.fori_loop(0, 32, bis, jnp.full((1, t), INT_MIN, jnp.int32))
    thr = jnp.maximum(thr, INT_MIN + 1)
    n_ge = count_ge(thr)

    @pl.when(jnp.max(n_ge) > topk)
    def _():
        n_gt = count_ge(thr + 1)
        quota = (topk - n_gt).astype(F32)
        tie_sc[...] = jnp.zeros(tie_sc.shape, F32)
        lower = jnp.where(qry_i < key_i, 1.0, 0.0).astype(BF16)

        def fix(jt, carry):
            key = key_sc[jt]
            eq = key == thr
            eqf = jnp.where(eq, 1.0, 0.0)
            rank = _nn(lower, eqf.astype(BF16)) + tie_sc[...]
            drop = jnp.logical_and(eq, rank >= quota)
            key_sc[jt] = jnp.where(drop, INT_MIN, key)
            tie_sc[...] = tie_sc[...] + jnp.sum(eqf, axis=0, keepdims=True)
            return carry
        lax.fori_loop(0, ntile, fix, 0)

    m_sc[...] = jnp.full(m_sc.shape, NEG_BIG, F32)
    l_sc[...] = jnp.zeros(l_sc.shape, F32)
    acc_sc[...] = jnp.zeros(acc_sc.shape, F32)

    def p3(jt, carry):
        sel = key_sc[jt] >= thr
        kk = key_rows(jt)
        vT = vdT_ref[0, jt]
        for h in range(DSA_HEADS):
            s = jnp.where(sel, _nn(kk, qz_sc[h]), NEG_BIG)
            m_prev = m_sc[h]
            m_cur = jnp.maximum(m_prev, jnp.max(s, axis=0, keepdims=True))
            alpha = jnp.exp2(m_prev - m_cur)
            p = jnp.exp2(s - m_cur)
            l_sc[h] = alpha * l_sc[h] + jnp.sum(p, axis=0, keepdims=True)
            acc_sc[h] = alpha * acc_sc[h] + _nn(vT, p.astype(BF16))
            m_sc[h] = m_cur
        return carry
    lax.fori_loop(0, ntile, p3, 0)
    o_ref[0] = jnp.concatenate(
        [acc_sc[h] / l_sc[h] for h in range(DSA_HEADS)], axis=0).T.astype(o_ref.dtype)


def dsa_attention(qdT, qiT, wiT, kdki, vdT, t):
    b, w, s = qdT.shape
    n = s // t
    topk = min(TOPK_MAX, s // 4)
    qspec = pl.BlockSpec((1, w, t), lambda bi, i: (bi, 0, i))
    return pl.pallas_call(
        functools.partial(_dsa_kernel, t=t, topk=topk),
        grid=(b, n),
        in_specs=[qspec, qspec, pl.BlockSpec((1, IDX_HEADS, t), lambda bi, i: (bi, 0, i)),
                  pl.BlockSpec((1, s, 128), lambda bi, i: (bi, 0, 0)),
                  pl.BlockSpec((1, n, DSA_HD, t), lambda bi, i: (bi, 0, 0, 0))],
        out_specs=pl.BlockSpec((1, t, w), lambda bi, i: (bi, i, 0)),
        out_shape=jax.ShapeDtypeStruct((b, s, w), BF16),
        scratch_shapes=[pltpu.VMEM((n, t, t), jnp.int32),
                        pltpu.VMEM((DSA_HEADS, 128, t), BF16), pltpu.VMEM((IDX_HEADS, 128, t), BF16),
                        pltpu.VMEM((1, t), F32),
                        pltpu.VMEM((DSA_HEADS, 1, t), F32), pltpu.VMEM((DSA_HEADS, 1, t), F32),
                        pltpu.VMEM((DSA_HEADS, DSA_HD, t), F32)],
        compiler_params=_cparams(("arbitrary", "arbitrary")),
        name="dsa_attention",
    )(qdT, qiT, wiT, kdki, vdT)


def _merge_kernel(oa_ref, ob_ref, oc_ref, od_ref, gt_ref, wb_ref, wo_ref, x_ref, mod_ref, pg_ref, o_ref,
                  *, gate_row):
    d = x_ref.shape[2]
    merged = None
    for bi_, o_r in enumerate((oa_ref, ob_ref, oc_ref, od_ref)):
        proj = _nn(o_r[0], wb_ref[bi_])
        gate = jax.nn.sigmoid(gt_ref[0, :, bi_ * d:(bi_ + 1) * d].astype(F32))
        merged = gate * proj if merged is None else merged + gate * proj
    y = _nn(merged.astype(BF16), wo_ref[...])
    yn = y * lax.rsqrt(jnp.mean(y * y, axis=-1, keepdims=True) + EPS) * pg_ref[...]
    o_ref[0] = x_ref[0] + mod_ref[0][gate_row:gate_row + 1] * yn


def merge_out(oa, ob, oc, od, gates, w_branch, w_out, x, mod6, post_g, tm=512):
    b, s, d = x.shape
    tm = min(tm, s)
    ospec = pl.BlockSpec((1, tm, BRANCH_W), lambda bi, i: (bi, i, 0))
    return pl.pallas_call(
        functools.partial(_merge_kernel, gate_row=2),
        grid=(b, s // tm),
        in_specs=[ospec, ospec, ospec, ospec,
                  pl.BlockSpec((1, tm, N_BRANCH * d), lambda bi, i: (bi, i, 0)),
                  pl.BlockSpec((N_BRANCH, BRANCH_W, d), lambda bi, i: (0, 0, 0)),
                  pl.BlockSpec((d, d), lambda bi, i: (0, 0)),
                  pl.BlockSpec((1, tm, d), lambda bi, i: (bi, i, 0)),
                  pl.BlockSpec((1, 6, d), lambda bi, i: (bi, 0, 0)),
                  pl.BlockSpec((1, d), lambda bi, i: (0, 0))],
        out_specs=pl.BlockSpec((1, tm, d), lambda bi, i: (bi, i, 0)),
        out_shape=jax.ShapeDtypeStruct((b, s, d), F32),
        compiler_params=_cparams(("arbitrary", "arbitrary")),
        name="merge_out",
    )(oa, ob, oc, od, gates, w_branch, w_out, x, mod6, post_g.reshape(1, d))


FFN_HALO = 16


def _ffn_kernel(x_ref, halo_ref, mod_ref, pre_g_ref, wg_ref, wv_ref, cwg_ref, cwv_ref, cbg_ref, cbv_ref,
                wd_ref, post_g_ref, o_ref, h_sc, acc_sc, *, tm):
    i = pl.program_id(1)
    c = pl.program_id(2)
    nc = pl.num_programs(2)
    mod = mod_ref[0]

    def modulated(xv):
        ms = jnp.mean(xv * xv, axis=-1, keepdims=True)
        y = xv * lax.rsqrt(ms + EPS) * pre_g_ref[...]
        return y * (1.0 + mod[4:5]) + mod[3:4]

    @pl.when(c == 0)
    def _():
        hh = modulated(halo_ref[0])
        h_sc[0:FFN_HALO, :] = jnp.where(i == 0, 0.0, hh).astype(BF16)
        h_sc[FFN_HALO:FFN_HALO + tm, :] = modulated(x_ref[0]).astype(BF16)
        acc_sc[...] = jnp.zeros(acc_sc.shape, F32)

    hext = h_sc[...]

    def conv(u, cw_ref, cb_ref):
        cw = cw_ref[...]
        out = cb_ref[...] + cw[2:3] * u[FFN_HALO:FFN_HALO + tm]
        out = out + cw[1:2] * u[FFN_HALO - 1:FFN_HALO - 1 + tm]
        return out + cw[0:1] * u[FFN_HALO - 2:FFN_HALO - 2 + tm]

    gate = conv(_nn(hext, wg_ref[...]), cwg_ref, cbg_ref)
    val = conv(_nn(hext, wv_ref[...]), cwv_ref, cbv_ref)
    act = (gate * jax.nn.sigmoid(gate) * val).astype(BF16)
    acc_sc[...] = acc_sc[...] + _nn(act, wd_ref[...])

    @pl.when(c == nc - 1)
    def _():
        y = acc_sc[...]
        yn = y * lax.rsqrt(jnp.mean(y * y, axis=-1, keepdims=True) + EPS) * post_g_ref[...]
        o_ref[0] = x_ref[0] + mod[5:6] * yn


def conv_ffn(x, mod6, pre_g, w_up, conv_w, conv_b, w_down, post_g, tm=512, cf=1408):
    b, s, d = x.shape
    dff = w_down.shape[0]
    tm = min(tm, s)
    nc = dff // cf
    r = tm // FFN_HALO
    cb2 = conv_b.reshape(1, 2 * dff)
    return pl.pallas_call(
        functools.partial(_ffn_kernel, tm=tm),
        grid=(b, s // tm, nc),
        in_specs=[pl.BlockSpec((1, tm, d), lambda bi, i, c: (bi, i, 0)),
                  pl.BlockSpec((1, FFN_HALO, d), lambda bi, i, c: (bi, jnp.maximum(i * r - 1, 0), 0)),
                  pl.BlockSpec((1, 6, d), lambda bi, i, c: (bi, 0, 0)),
                  pl.BlockSpec((1, d), lambda bi, i, c: (0, 0)),
                  pl.BlockSpec((d, cf), lambda bi, i, c: (0, c)),
                  pl.BlockSpec((d, cf), lambda bi, i, c: (0, nc + c)),
                  pl.BlockSpec((FFN_CONV_K, cf), lambda bi, i, c: (0, c)),
                  pl.BlockSpec((FFN_CONV_K, cf), lambda bi, i, c: (0, nc + c)),
                  pl.BlockSpec((1, cf), lambda bi, i, c: (0, c)),
                  pl.BlockSpec((1, cf), lambda bi, i, c: (0, nc + c)),
                  pl.BlockSpec((cf, d), lambda bi, i, c: (c, 0)),
                  pl.BlockSpec((1, d), lambda bi, i, c: (0, 0))],
        out_specs=pl.BlockSpec((1, tm, d), lambda bi, i, c: (bi, i, 0)),
        out_shape=jax.ShapeDtypeStruct((b, s, d), F32),
        scratch_shapes=[pltpu.VMEM((tm + FFN_HALO, d), BF16), pltpu.VMEM((tm, d), F32)],
        compiler_params=_cparams(("arbitrary", "arbitrary", "arbitrary")),
        name="conv_ffn",
    )(x, x, mod6, pre_g.reshape(1, d), w_up, w_up, conv_w, conv_w, cb2, cb2, w_down, post_g.reshape(1, d))


def _reorder_w_in(w_in_l):
    sizes = (512, 256, 256, 256, 256, 256, 256, 256, 64, 64, 256, 32, 8, GATES_W)
    offs = [0]
    for sz in sizes:
        offs.append(offs[-1] + sz)
    col = lambda idx: w_in_l[:, offs[idx]:offs[idx + 1]]
    d = w_in_l.shape[0]
    misc = jnp.concatenate([col(8), col(9), col(11), col(12), jnp.zeros((d, 256 - 168), w_in_l.dtype)], axis=1)
    main = jnp.concatenate([col(0), col(1), col(2), col(3), col(4), col(5), col(6), col(7), misc, col(10)],
                           axis=1)
    return main.astype(BF16), col(13).astype(BF16)


def kernel(x, c, positions, ada_w, ada_b, mix_pre_g, mix_post_g, ffn_pre_g, ffn_post_g, w_in, conv_a_w,
           conv_a_b, conv_a_ln_g, conv_a_ln_b, lam_q1, lam_k1, lam_q2, lam_k2, diff_subln_g, w_branch,
           w_out, w_up, ffn_conv_w, ffn_conv_b, w_down):
    depth = ada_w.shape[0]
    b, s, d = x.shape
    t = min(ATT_T, s)
    mod_all = adaln_mod(c, ada_w, ada_b)
    tabs = rope_tables(positions)
    for l in range(depth):
        lam_init = 0.8 - 0.6 * math.exp(-0.3 * l)
        mod6 = mod_all[l].reshape(b, 6, d)
        w_main, w_gates = _reorder_w_in(w_in[l])
        zmain = norm_mod_matmul(x, mod6, mix_pre_g[l], w_main, F32, 0, 1, name="in_proj_main")
        gates = norm_mod_matmul(x, mod6, mix_pre_g[l], w_gates, BF16, 0, 1, name="in_proj_gates")
        qbT, kb, vbT, qcT, kc, vcT, qdT, qiT, kdki, vdT, wiT = prep(zmain, tabs, t)
        oa = conformer(zmain, conv_a_w[l], conv_a_b[l], conv_a_ln_g[l], conv_a_ln_b[l])
        ob = diff_attention(qbT, kb, vbT, lam_q1[l], lam_k1[l], lam_q2[l], lam_k2[l], diff_subln_g[l],
                            lam_init, t)
        oc = sb_attention(qcT, kc, vcT, t)
        od = dsa_attention(qdT, qiT, wiT, kdki, vdT, t)
        x = merge_out(oa, ob, oc, od, gates, w_branch[l].astype(BF16), w_out[l].astype(BF16), x, mod6,
                      mix_post_g[l])
        x = conv_ffn(x, mod6, ffn_pre_g[l], w_up[l].astype(BF16), ffn_conv_w[l], ffn_conv_b[l],
                     w_down[l].astype(BF16), ffn_post_g[l])
    return x
```

```python
import functools
import math

import jax
import jax.numpy as jnp
from jax import lax
from jax.experimental import pallas as pl
from jax.experimental.pallas import tpu as pltpu

F32 = jnp.float32
BF16 = jnp.bfloat16

N_BRANCH = 4
BRANCH_W = 256
ROPE_THETA = 10000.0
EPS = 1e-6
CONV_CH = 256
CONV_K = 31
DIFF_VD = 64
DIFF_QK = 32
DIFF_HEADS = 4
SB_HD = 64
SB_HEADS = 4
DSA_HD = 64
DSA_HEADS = 4
IDX_HEADS = 8
IDX_HD = 32
TOPK_MAX = 256
FFN_CONV_K = 3

ZMAIN_W = 2816
GATES_W = N_BRANCH * 1024
MISC_KD, MISC_VD, MISC_KI, MISC_WI = 0, 64, 128, 160

VMEM_LIMIT = 56 * 1024 * 1024
INT_MIN = -(2 ** 31)
I16_MIN, I16_MAX = -(2 ** 15), 2 ** 15 - 1
NEG_BIG = -1e30
LOG2E = math.log2(math.e)
ATT_T = 512
ONES_ROWS = 16


def _cparams(sem):
    return pltpu.CompilerParams(dimension_semantics=sem, vmem_limit_bytes=VMEM_LIMIT)


def _nn(a, b):
    return jnp.dot(a, b, preferred_element_type=F32)


def _mod_kernel(c_ref, w_ref, b_ref, o_ref):
    c = c_ref[...]
    c_act = c * jax.nn.sigmoid(c)
    o_ref[0] = _nn(c_act, w_ref[0]) + b_ref[0]


def adaln_mod(c, ada_w, ada_b):
    depth, d, n = ada_w.shape
    b = c.shape[0]
    tn = 1024
    return pl.pallas_call(
        _mod_kernel,
        grid=(depth, n // tn),
        in_specs=[pl.BlockSpec((b, d), lambda l, j: (0, 0)),
                  pl.BlockSpec((1, d, tn), lambda l, j: (l, 0, j)),
                  pl.BlockSpec((1, 1, tn), lambda l, j: (l, 0, j))],
        out_specs=pl.BlockSpec((1, b, tn), lambda l, j: (l, 0, j)),
        out_shape=jax.ShapeDtypeStruct((depth, b, n), F32),
        compiler_params=_cparams(("arbitrary", "arbitrary")),
        name="adaln_mod",
    )(c, ada_w, ada_b.reshape(depth, 1, n))


def _rope_table_kernel(pos_ref, f_ref, c32_ref, s32_ref, c64_ref, s64_ref):
    pos = pos_ref[0].astype(F32)
    f = f_ref[...]
    a32 = pos * f[0:1]
    a64 = pos * f[2:3]
    c32_ref[0] = jnp.cos(a32)
    s32_ref[0] = jnp.sin(a32) * f[1:2]
    c64_ref[0] = jnp.cos(a64)
    s64_ref[0] = jnp.sin(a64) * f[3:4]


def rope_tables(positions, ts=512):
    b, s = positions.shape
    ts = min(ts, s)

    def pattern(d):
        inv = ROPE_THETA ** (-jnp.arange(0, d, 2, dtype=F32) / d)
        freq = jnp.tile(jnp.concatenate([inv, inv]), 128 // d)
        sign = jnp.tile(jnp.concatenate([-jnp.ones(d // 2, F32), jnp.ones(d // 2, F32)]), 128 // d)
        return freq, sign

    f32_, s32_ = pattern(IDX_HD)
    f64_, s64_ = pattern(DSA_HD)
    ftab = jnp.stack([f32_, s32_, f64_, s64_])
    tab = jax.ShapeDtypeStruct((b, s, 128), F32)
    spec = pl.BlockSpec((1, ts, 128), lambda bi, i: (bi, i, 0))
    return pl.pallas_call(
        _rope_table_kernel,
        grid=(b, s // ts),
        in_specs=[pl.BlockSpec((1, ts, 1), lambda bi, i: (bi, i, 0)),
                  pl.BlockSpec((4, 128), lambda bi, i: (0, 0))],
        out_specs=[spec] * 4,
        out_shape=[tab] * 4,
        compiler_params=_cparams(("arbitrary", "arbitrary")),
        name="rope_tables",
    )(positions.reshape(b, s, 1), ftab)


def _nmm_kernel(x_ref, mod_ref, g_ref, w_ref, o_ref, *, shift_row, scale_row):
    x = x_ref[0]
    mod = mod_ref[0]
    ms = jnp.mean(x * x, axis=-1, keepdims=True)
    y = x * lax.rsqrt(ms + EPS) * g_ref[...]
    h = y * (1.0 + mod[scale_row:scale_row + 1]) + mod[shift_row:shift_row + 1]
    o_ref[0] = _nn(h.astype(BF16), w_ref[...]).astype(o_ref.dtype)


def norm_mod_matmul(x, mod6, g, w, out_dtype, shift_row, scale_row, tm=512, name="nmm"):
    b, s, d = x.shape
    n = w.shape[1]
    tm = min(tm, s)
    return pl.pallas_call(
        functools.partial(_nmm_kernel, shift_row=shift_row, scale_row=scale_row),
        grid=(b, s // tm),
        in_specs=[pl.BlockSpec((1, tm, d), lambda bi, i: (bi, i, 0)),
                  pl.BlockSpec((1, 6, d), lambda bi, i: (bi, 0, 0)),
                  pl.BlockSpec((1, d), lambda bi, i: (0, 0)),
                  pl.BlockSpec((d, n), lambda bi, i: (0, 0))],
        out_specs=pl.BlockSpec((1, tm, n), lambda bi, i: (bi, i, 0)),
        out_shape=jax.ShapeDtypeStruct((b, s, n), out_dtype),
        compiler_params=_cparams(("arbitrary", "arbitrary")),
        name=name,
    )(x, mod6, g.reshape(1, d), w)


def _rope(x, cos, sin_signed, half):
    w = x.shape[1]
    lane = lax.broadcasted_iota(jnp.int32, x.shape, 1)
    first = (lane % (2 * half)) < half
    swapped = jnp.where(first, pltpu.roll(x, w - half, 1), pltpu.roll(x, half, 1))
    return x * cos + swapped * sin_signed


def _inproj_kernel(x_ref, mod_ref, g_ref, w_ref, c32_ref, s32_ref, c64_ref, s64_ref,
                   oua, oqbT, okb, ovbT, oqcT, okc, ovcT, oqdT, oqiT, okdki, ovdT, owiT):
    x = x_ref[0]
    mod = mod_ref[0]
    y = x * lax.rsqrt(jnp.mean(x * x, axis=-1, keepdims=True) + EPS) * g_ref[...]
    h = y * (1.0 + mod[1:2]) + mod[0:1]
    z = _nn(h.astype(BF16), w_ref[...])
    col = lambda cb: z[:, cb * 256:(cb + 1) * 256]
    oua[0] = z[:, 0:2 * CONV_CH]
    c32 = jnp.concatenate([c32_ref[0]] * 2, axis=1)
    s32 = jnp.concatenate([s32_ref[0]] * 2, axis=1)
    c64 = jnp.concatenate([c64_ref[0]] * 2, axis=1)
    s64 = jnp.concatenate([s64_ref[0]] * 2, axis=1)
    h32, h64 = IDX_HD // 2, DSA_HD // 2
    oqbT[0] = (_rope(col(2), c32, s32, h32) * (DIFF_QK ** -0.5 * LOG2E)).T.astype(BF16)
    okb[0] = _rope(col(3), c32, s32, h32).astype(BF16)
    ovbT[0] = col(4).T.astype(BF16)
    oqcT[0] = (col(5) * (SB_HD ** -0.5 * LOG2E)).T.astype(BF16)
    okc[0] = col(6).astype(BF16)
    ovcT[0] = col(7).T.astype(BF16)
    oqdT[0] = (_rope(col(8), c64, s64, h64) * (DSA_HD ** -0.5 * LOG2E)).T.astype(BF16)
    oqiT[0] = _rope(col(10), c32, s32, h32).T.astype(BF16)
    md = col(9)
    md64 = _rope(md, c64, s64, h64)
    md32 = _rope(md, c32, s32, h32)
    ts = md.shape[0]
    okdki[0] = jnp.concatenate(
        [md64[:, MISC_KD:MISC_KD + DSA_HD], md32[:, MISC_KI:MISC_KI + IDX_HD],
         jnp.zeros((ts, 128 - DSA_HD - IDX_HD), F32)], axis=1).astype(BF16)
    mdT = md.T
    ovdT[0, 0] = mdT[MISC_VD:MISC_VD + DSA_HD].astype(BF16)
    owiT[0] = mdT[MISC_WI:MISC_WI + IDX_HEADS] * (IDX_HEADS ** -0.5 * IDX_HD ** -0.5)


def in_proj(x, mod6, g, w_main, tabs, ts):
    b, s, d = x.shape
    tspec = pl.BlockSpec((1, ts, 128), lambda bi, i: (bi, i, 0))
    row = pl.BlockSpec((1, ts, 256), lambda bi, i: (bi, i, 0))
    colT = pl.BlockSpec((1, 256, ts), lambda bi, i: (bi, 0, i))
    s_row = jax.ShapeDtypeStruct((b, s, 256), BF16)
    s_colT = jax.ShapeDtypeStruct((b, 256, s), BF16)
    out_specs = [pl.BlockSpec((1, ts, 2 * CONV_CH), lambda bi, i: (bi, i, 0)),
                 colT, row, colT, colT, row, colT, colT, colT,
                 pl.BlockSpec((1, ts, 128), lambda bi, i: (bi, i, 0)),
                 pl.BlockSpec((1, 1, DSA_HD, ts), lambda bi, i: (bi, i, 0, 0)),
                 pl.BlockSpec((1, IDX_HEADS, ts), lambda bi, i: (bi, 0, i))]
    out_shape = [jax.ShapeDtypeStruct((b, s, 2 * CONV_CH), F32),
                 s_colT, s_row, s_colT, s_colT, s_row, s_colT, s_colT, s_colT,
                 jax.ShapeDtypeStruct((b, s, 128), BF16),
                 jax.ShapeDtypeStruct((b, s // ts, DSA_HD, ts), BF16),
                 jax.ShapeDtypeStruct((b, IDX_HEADS, s), F32)]
    return pl.pallas_call(
        _inproj_kernel,
        grid=(b, s // ts),
        in_specs=[pl.BlockSpec((1, ts, d), lambda bi, i: (bi, i, 0)),
                  pl.BlockSpec((1, 6, d), lambda bi, i: (bi, 0, 0)),
                  pl.BlockSpec((1, d), lambda bi, i: (0, 0)),
                  pl.BlockSpec((d, ZMAIN_W), lambda bi, i: (0, 0))] + [tspec] * 4,
        out_specs=out_specs,
        out_shape=out_shape,
        compiler_params=_cparams(("arbitrary", "arbitrary")),
        name="in_proj",
    )(x, mod6, g.reshape(1, d), w_main, *tabs)


def _masked_rows(xT, lo, hi):
    r = lax.broadcasted_iota(jnp.int32, xT.shape, 0)
    return jnp.where(jnp.logical_and(r >= lo, r < hi), xT, jnp.zeros_like(xT))


def _causal_pairs(n, descending=False):
    pairs = [(i, j) for i in range(n) for j in (range(i, -1, -1) if descending else range(i + 1))]
    return (jnp.array([p[0] for p in pairs], jnp.int32), jnp.array([p[1] for p in pairs], jnp.int32))


CONV_HALO = 32


def _conformer_kernel(u_ref, halo_ref, w_ref, b_ref, g_ref, beta_ref, o_ref, buf, *, ts):
    i = pl.program_id(1)
    u = u_ref[0]
    buf[CONV_HALO:CONV_HALO + ts, :] = u[:, :CONV_CH] * jax.nn.sigmoid(u[:, CONV_CH:])
    uh = halo_ref[0]
    hh = uh[:, :CONV_CH] * jax.nn.sigmoid(uh[:, CONV_CH:])
    buf[0:CONV_HALO, :] = jnp.where(i == 0, 0.0, hh)
    w = w_ref[...]
    acc = jnp.zeros((ts, CONV_CH), F32) + b_ref[...]
    for k in range(CONV_K):
        acc = acc + w[k:k + 1] * buf[pl.ds(CONV_HALO - (CONV_K - 1) + k, ts), :]
    mu = jnp.mean(acc, axis=-1, keepdims=True)
    xc = acc - mu
    y = xc * lax.rsqrt(jnp.mean(xc * xc, axis=-1, keepdims=True) + EPS) * g_ref[...] + beta_ref[...]
    o_ref[0] = (y * jax.nn.sigmoid(y)).astype(o_ref.dtype)


def conformer(ua, conv_w, conv_b, ln_g, ln_b, ts=512):
    b, s, _ = ua.shape
    ts = min(ts, s)
    r = ts // CONV_HALO
    wpad = jnp.concatenate([conv_w, jnp.zeros((32 - CONV_K, CONV_CH), F32)], axis=0)
    vec = lambda a: a.reshape(1, CONV_CH)
    vspec = pl.BlockSpec((1, CONV_CH), lambda bi, i: (0, 0))
    return pl.pallas_call(
        functools.partial(_conformer_kernel, ts=ts),
        grid=(b, s // ts),
        in_specs=[pl.BlockSpec((1, ts, 2 * CONV_CH), lambda bi, i: (bi, i, 0)),
                  pl.BlockSpec((1, CONV_HALO, 2 * CONV_CH), lambda bi, i: (bi, jnp.maximum(i * r - 1, 0), 0)),
                  pl.BlockSpec((32, CONV_CH), lambda bi, i: (0, 0)),
                  vspec, vspec, vspec],
        out_specs=pl.BlockSpec((1, ts, CONV_CH), lambda bi, i: (bi, i, 0)),
        out_shape=jax.ShapeDtypeStruct((b, s, CONV_CH), BF16),
        scratch_shapes=[pltpu.VMEM((ts + CONV_HALO, CONV_CH), F32)],
        compiler_params=_cparams(("arbitrary", "arbitrary")),
        name="conformer",
    )(ua, ua, wpad, vec(conv_b), vec(ln_g), vec(ln_b))


def _diff_kernel(qi_tab, kj_tab, qT_ref, k_ref, vT_ref, lq1, lk1, lq2, lk2, sg_ref, o_ref, qz_sc, m_sc, acc_sc,
                 *, t, lam_init):
    i = qi_tab[pl.program_id(1)]
    j = kj_tab[pl.program_id(1)]
    nmaps = 2 * DIFF_HEADS

    @pl.when(j == 0)
    def _():
        m_sc[...] = jnp.full(m_sc.shape, -jnp.inf, F32)
        acc_sc[...] = jnp.zeros(acc_sc.shape, F32)
        qT = qT_ref[0].astype(F32)
        for jj in range(nmaps):
            g, r = divmod(jj * DIFF_QK, 128)
            qz_sc[jj] = _masked_rows(qT[g * 128:(g + 1) * 128], r, r + DIFF_QK).astype(BF16)

    def step(masked):
        k = k_ref[0]
        vT = vT_ref[0]
        ones = jnp.ones((ONES_ROWS, t), BF16)
        v_ones = [jnp.concatenate([vT[h * DIFF_VD:(h + 1) * DIFF_VD], ones], axis=0)
                  for h in range(DIFF_HEADS)]
        if masked:
            key_i = lax.broadcasted_iota(jnp.int32, (t, t), 0)
            qry_i = lax.broadcasted_iota(jnp.int32, (t, t), 1)
            causal = key_i <= qry_i
        def logits(jj):
            g = (jj * DIFF_QK) // 128
            return _nn(k[:, g * 128:(g + 1) * 128], qz_sc[jj])

        ahead = 2
        queue = [logits(jj) for jj in range(ahead)]
        for jj in range(nmaps):
            h = jj // 2
            s = queue.pop(0)
            if jj + ahead < nmaps:
                queue.append(logits(jj + ahead))
            if masked:
                s = jnp.where(causal, s, -jnp.inf)
            m_prev = m_sc[jj]
            m_cur = jnp.maximum(m_prev, jnp.max(s, axis=0, keepdims=True))
            alpha = jnp.exp2(m_prev - m_cur)
            p = jnp.exp2(s - m_cur)
            acc_sc[jj] = alpha * acc_sc[jj] + _nn(v_ones[h], p.astype(BF16))
            m_sc[jj] = m_cur

    @pl.when(j < i)
    def _():
        step(False)

    @pl.when(j == i)
    def _():
        step(True)
        lam = (jnp.exp(jnp.sum(lq1[...].astype(F32) * lk1[...].astype(F32), axis=1, keepdims=True))
               - jnp.exp(jnp.sum(lq2[...].astype(F32) * lk2[...].astype(F32), axis=1, keepdims=True))
               + lam_init)
        outs = []
        for h in range(DIFF_HEADS):
            a0 = acc_sc[2 * h]
            a1 = acc_sc[2 * h + 1]
            o0 = a0[0:DIFF_VD] / a0[DIFF_VD:DIFF_VD + 1]
            o1 = a1[0:DIFF_VD] / a1[DIFF_VD:DIFF_VD + 1]
            o = o0 - lam * o1
            o = o * lax.rsqrt(jnp.mean(o * o, axis=0, keepdims=True) + EPS) * sg_ref[...]
            outs.append(o * (1.0 - lam_init))
        o_ref[0] = jnp.concatenate(outs, axis=0).T.astype(o_ref.dtype)


def diff_attention(qbT, kb, vbT, lq1, lk1, lq2, lk2, subln_g, lam_init, t):
    b, s, w = kb.shape
    n = s // t
    nmaps = 2 * DIFF_HEADS
    qi_tab, kj_tab = _causal_pairs(n)
    qspec = pl.BlockSpec((1, w, t), lambda bi, p, qi, kj: (bi, 0, qi[p]))
    kspec = pl.BlockSpec((1, t, w), lambda bi, p, qi, kj: (bi, kj[p], 0))
    vspec = pl.BlockSpec((1, w, t), lambda bi, p, qi, kj: (bi, 0, kj[p]))
    pspec = pl.BlockSpec((1, DIFF_QK), lambda bi, p, qi, kj: (0, 0))
    return pl.pallas_call(
        functools.partial(_diff_kernel, t=t, lam_init=lam_init),
        grid_spec=pltpu.PrefetchScalarGridSpec(
            num_scalar_prefetch=2,
            grid=(b, qi_tab.shape[0]),
            in_specs=[qspec, kspec, vspec, pspec, pspec, pspec, pspec,
                      pl.BlockSpec((DIFF_VD, 1), lambda bi, p, qi, kj: (0, 0))],
            out_specs=pl.BlockSpec((1, t, w), lambda bi, p, qi, kj: (bi, qi[p], 0)),
            scratch_shapes=[pltpu.VMEM((nmaps, 128, t), BF16),
                            pltpu.VMEM((nmaps, 1, t), F32),
                            pltpu.VMEM((nmaps, DIFF_VD + ONES_ROWS, t), F32)]),
        out_shape=jax.ShapeDtypeStruct((b, s, w), BF16),
        compiler_params=_cparams(("arbitrary", "arbitrary")),
        name="diff_attention",
    )(qi_tab, kj_tab, qbT, kb, vbT, lq1.reshape(1, -1), lk1.reshape(1, -1), lq2.reshape(1, -1),
      lk2.reshape(1, -1), subln_g.reshape(-1, 1))


SB_BLK = 128


def _sb_kernel(qi_tab, kj_tab, qT_ref, k_ref, vT_ref, o_ref, qz_sc, c_sc, acc_sc, *, t):
    i = qi_tab[pl.program_id(1)]
    j = i - kj_tab[pl.program_id(1)]

    @pl.when(j == 0)
    def _():
        c_sc[...] = jnp.zeros(c_sc.shape, F32)
        acc_sc[...] = jnp.zeros(acc_sc.shape, F32)
        qT = qT_ref[0].astype(F32)
        for h in range(SB_HEADS):
            g, r = divmod(h * SB_HD, 128)
            qz_sc[h] = _masked_rows(qT[g * 128:(g + 1) * 128], r, r + SB_HD).astype(BF16)

    def step(masked):
        k = k_ref[0]
        vT = vT_ref[0]
        ra = lax.broadcasted_iota(jnp.int32, (SB_BLK, SB_BLK), 0)
        rb = lax.broadcasted_iota(jnp.int32, (SB_BLK, SB_BLK), 1)
        upper = jnp.where(rb > ra, 1.0, 0.0).astype(BF16)
        upper2 = jnp.concatenate([upper, upper], axis=1)
        if masked:
            key_i = lax.broadcasted_iota(jnp.int32, (t, t), 0)
            qry_i = lax.broadcasted_iota(jnp.int32, (t, t), 1)
            strict = key_i < qry_i
        def logits(h):
            g = (h * SB_HD) // 128
            return _nn(k[:, g * 128:(g + 1) * 128], qz_sc[h])

        z_next = logits(0)
        for h in range(SB_HEADS):
            z = z_next
            if h + 1 < SB_HEADS:
                z_next = logits(h + 1)
            neg_abs = lax.bitcast_convert_type(
                lax.bitcast_convert_type(z, jnp.uint32) | jnp.uint32(0x80000000), F32)
            ls_pos = jnp.minimum(z, 0.0) - jnp.log(1.0 + jnp.exp2(neg_abs)) * LOG2E
            keep = ls_pos - z
            if masked:
                keep = jnp.where(strict, keep, 0.0)
            hi = keep.astype(BF16)
            lo = (keep - hi.astype(F32)).astype(BF16)
            carry = c_sc[h]
            parts = []
            for blk in range(t // SB_BLK - 1, -1, -1):
                sl = slice(blk * SB_BLK, (blk + 1) * SB_BLK)
                suffix = _nn(upper2, jnp.concatenate([hi[sl], lo[sl]], axis=0))
                parts.append(suffix + carry)
                carry = carry + suffix[0:1] + keep[blk * SB_BLK:blk * SB_BLK + 1]
            c_sc[h] = carry
            later = jnp.concatenate(parts[::-1], axis=0)
            wgt = jnp.exp2(ls_pos + later)
            if masked:
                wgt = jnp.where(strict, wgt, 0.0)
            acc_sc[h] = acc_sc[h] + _nn(vT[h * SB_HD:(h + 1) * SB_HD], wgt.astype(BF16))

    @pl.when(j == 0)
    def _():
        step(True)

    @pl.when(jnp.logical_and(j > 0, j <= i))
    def _():
        step(False)

    @pl.when(j == i)
    def _():
        o_ref[0] = jnp.concatenate([acc_sc[h] for h in range(SB_HEADS)], axis=0).T.astype(o_ref.dtype)


def sb_attention(qcT, kc, vcT, t):
    b, s, w = kc.shape
    n = s // t
    qi_tab, kj_tab = _causal_pairs(n, descending=True)
    qspec = pl.BlockSpec((1, w, t), lambda bi, p, qi, kj: (bi, 0, qi[p]))
    kspec = pl.BlockSpec((1, t, w), lambda bi, p, qi, kj: (bi, kj[p], 0))
    vspec = pl.BlockSpec((1, w, t), lambda bi, p, qi, kj: (bi, 0, kj[p]))
    return pl.pallas_call(
        functools.partial(_sb_kernel, t=t),
        grid_spec=pltpu.PrefetchScalarGridSpec(
            num_scalar_prefetch=2,
            grid=(b, qi_tab.shape[0]),
            in_specs=[qspec, kspec, vspec],
            out_specs=pl.BlockSpec((1, t, w), lambda bi, p, qi, kj: (bi, qi[p], 0)),
            scratch_shapes=[pltpu.VMEM((SB_HEADS, 128, t), BF16),
                            pltpu.VMEM((SB_HEADS, 1, t), F32), pltpu.VMEM((SB_HEADS, SB_HD, t), F32)]),
        out_shape=jax.ShapeDtypeStruct((b, s, w), BF16),
        compiler_params=_cparams(("arbitrary", "arbitrary")),
        name="sb_attention",
    )(qi_tab, kj_tab, qcT, kc, vcT)


def _dsa_kernel(qdT_ref, qiT_ref, wiT_ref, kdki_ref, vdT_ref, o_ref,
                key_sc, half_sc, qz_sc, qiz_sc, tie_sc, m_sc, acc_sc, *, t, topk):
    i = pl.program_id(1)
    ntile = i + 1
    wiT = wiT_ref[0]

    qdT = qdT_ref[0].astype(F32)
    zeros64 = jnp.zeros((128 - DSA_HD, t), F32)
    for h in range(DSA_HEADS):
        qz_sc[h] = jnp.concatenate([qdT[h * DSA_HD:(h + 1) * DSA_HD], zeros64], axis=0).astype(BF16)
    qiT = qiT_ref[0].astype(F32)
    for h in range(IDX_HEADS):
        qiz_sc[h] = jnp.concatenate(
            [jnp.zeros((DSA_HD, t), F32), qiT[h * IDX_HD:(h + 1) * IDX_HD],
             jnp.zeros((128 - DSA_HD - IDX_HD, t), F32)], axis=0).astype(BF16)

    def key_rows(jt):
        return kdki_ref[0, pl.ds(pl.multiple_of(jt * t, t), t), :]

    def score_keys(jt):
        kk = key_rows(jt)
        acc = jnp.zeros((t, t), F32)
        for h in range(IDX_HEADS):
            acc = acc + jnp.maximum(_nn(kk, qiz_sc[h]), 0.0) * wiT[h:h + 1]
        bits = lax.bitcast_convert_type(acc, jnp.int32)
        key = jnp.where(bits < 0, bits ^ jnp.int32(0x7FFFFFFF), bits)
        return jnp.where(acc == 0.0, 0, key)

    def p1(jt, carry):
        key_sc[jt] = score_keys(jt)
        return carry
    lax.fori_loop(0, i, p1, 0)
    key_i = lax.broadcasted_iota(jnp.int32, (t, t), 0)
    qry_i = lax.broadcasted_iota(jnp.int32, (t, t), 1)
    key_sc[i] = jnp.where(key_i <= qry_i, score_keys(i), INT_MIN)

    def count_ge(cand):
        def body(jt, part):
            ge = jnp.where(key_sc[jt] >= cand, 1, 0)
            return part + jnp.sum(ge.reshape(t // 8, 8, t), axis=0)
        part = lax.fori_loop(0, ntile, body, jnp.zeros((8, t), jnp.int32))
        return jnp.sum(part, axis=0, keepdims=True)

    def count16_ge(cand):
        cand16 = cand.astype(jnp.int16)

        def body(jt, acc):
            ge = jnp.where(half_sc[jt] >= cand16, jnp.bfloat16(1), jnp.bfloat16(0))
            g3 = ge.reshape(t // 16, 16, t)
            part = g3[0]
            for r in range(1, t // 16):
                part = part + g3[r]
            return acc + part.astype(F32)
        acc = lax.fori_loop(0, ntile, body, jnp.zeros((16, t), F32))
        return jnp.sum(acc, axis=0, keepdims=True)

    def search16(quota):
        def bis(it, thr):
            cand = thr + lax.shift_left(jnp.int32(1), jnp.int32(15) - it)
            return jnp.where(count16_ge(cand) >= quota, cand, thr)
        return lax.fori_loop(0, 16, bis, jnp.full((1, t), I16_MIN, jnp.int32))

    def fill_hi(jt, carry):
        half_sc[jt] = lax.shift_right_arithmetic(key_sc[jt], 16).astype(jnp.int16)
        return carry
    lax.fori_loop(0, ntile, fill_hi, 0)
    thr_hi = search16(jnp.float32(topk))
    above = jnp.where(thr_hi < I16_MAX, count16_ge(jnp.minimum(thr_hi + 1, I16_MAX)), 0.0)

    def fill_lo(jt, carry):
        key = key_sc[jt]
        low = (key & 0xFFFF) + I16_MIN
        in_group = lax.shift_right_arithmetic(key, 16) == thr_hi
        half_sc[jt] = jnp.where(in_group, low, I16_MIN).astype(jnp.int16)
        return carry
    lax.fori_loop(0, ntile, fill_lo, 0)
    thr_lo = search16(topk - above)
    thr = lax.shift_left(thr_hi, 16) | (thr_lo - I16_MIN)
    thr = jnp.maximum(thr, INT_MIN + 1)
    n_ge = count_ge(thr)

    @pl.when(jnp.max(n_ge) > topk)
    def _():
        n_gt = count_ge(thr + 1)
        quota = (topk - n_gt).astype(F32)
        tie_sc[...] = jnp.zeros(tie_sc.shape, F32)
        lower = jnp.where(qry_i < key_i, 1.0, 0.0).astype(BF16)

        def fix(jt, carry):
            key = key_sc[jt]
            eq = key == thr
            eqf = jnp.where(eq, 1.0, 0.0)
            rank = _nn(lower, eqf.astype(BF16)) + tie_sc[...]
            drop = jnp.logical_and(eq, rank >= quota)
            key_sc[jt] = jnp.where(drop, INT_MIN, key)
            tie_sc[...] = tie_sc[...] + jnp.sum(eqf, axis=0, keepdims=True)
            return carry
        lax.fori_loop(0, ntile, fix, 0)

    m_sc[...] = jnp.full(m_sc.shape, NEG_BIG, F32)
    acc_sc[...] = jnp.zeros(acc_sc.shape, F32)

    def p3(jt, carry):
        sel = key_sc[jt] >= thr
        kk = key_rows(jt)
        vT = jnp.concatenate([vdT_ref[0, jt], jnp.ones((ONES_ROWS, t), BF16)], axis=0)
        s_next = _nn(kk, qz_sc[0])
        for h in range(DSA_HEADS):
            s = jnp.where(sel, s_next, NEG_BIG)
            if h + 1 < DSA_HEADS:
                s_next = _nn(kk, qz_sc[h + 1])
            m_prev = m_sc[h]
            m_cur = jnp.maximum(m_prev, jnp.max(s, axis=0, keepdims=True))
            alpha = jnp.exp2(m_prev - m_cur)
            p = jnp.exp2(s - m_cur)
            acc_sc[h] = alpha * acc_sc[h] + _nn(vT, p.astype(BF16))
            m_sc[h] = m_cur
        return carry
    lax.fori_loop(0, ntile, p3, 0)
    o_ref[0] = jnp.concatenate(
        [acc_sc[h][0:DSA_HD] / acc_sc[h][DSA_HD:DSA_HD + 1] for h in range(DSA_HEADS)],
        axis=0).T.astype(o_ref.dtype)


def dsa_attention(qdT, qiT, wiT, kdki, vdT, t):
    b, w, s = qdT.shape
    n = s // t
    topk = min(TOPK_MAX, s // 4)
    qspec = pl.BlockSpec((1, w, t), lambda bi, i: (bi, 0, i))
    return pl.pallas_call(
        functools.partial(_dsa_kernel, t=t, topk=topk),
        grid=(b, n),
        in_specs=[qspec, qspec, pl.BlockSpec((1, IDX_HEADS, t), lambda bi, i: (bi, 0, i)),
                  pl.BlockSpec((1, s, 128), lambda bi, i: (bi, 0, 0)),
                  pl.BlockSpec((1, n, DSA_HD, t), lambda bi, i: (bi, 0, 0, 0))],
        out_specs=pl.BlockSpec((1, t, w), lambda bi, i: (bi, i, 0)),
        out_shape=jax.ShapeDtypeStruct((b, s, w), BF16),
        scratch_shapes=[pltpu.VMEM((n, t, t), jnp.int32), pltpu.VMEM((n, t, t), jnp.int16),
                        pltpu.VMEM((DSA_HEADS, 128, t), BF16), pltpu.VMEM((IDX_HEADS, 128, t), BF16),
                        pltpu.VMEM((1, t), F32),
                        pltpu.VMEM((DSA_HEADS, 1, t), F32),
                        pltpu.VMEM((DSA_HEADS, DSA_HD + ONES_ROWS, t), F32)],
        compiler_params=_cparams(("arbitrary", "arbitrary")),
        name="dsa_attention",
    )(qdT, qiT, wiT, kdki, vdT)


def _merge_kernel(oa_ref, ob_ref, oc_ref, od_ref, gt_ref, wb_ref, wo_ref, x_ref, mod_ref, pg_ref, o_ref,
                  *, gate_row):
    d = x_ref.shape[2]
    merged = None
    for bi_, o_r in enumerate((oa_ref, ob_ref, oc_ref, od_ref)):
        proj = _nn(o_r[0], wb_ref[bi_])
        gate = jax.nn.sigmoid(gt_ref[0, :, bi_ * d:(bi_ + 1) * d].astype(F32))
        merged = gate * proj if merged is None else merged + gate * proj
    y = _nn(merged.astype(BF16), wo_ref[...])
    yn = y * lax.rsqrt(jnp.mean(y * y, axis=-1, keepdims=True) + EPS) * pg_ref[...]
    o_ref[0] = x_ref[0] + mod_ref[0][gate_row:gate_row + 1] * yn


def merge_out(oa, ob, oc, od, gates, w_branch, w_out, x, mod6, post_g, tm=512):
    b, s, d = x.shape
    tm = min(tm, s)
    ospec = pl.BlockSpec((1, tm, BRANCH_W), lambda bi, i: (bi, i, 0))
    return pl.pallas_call(
        functools.partial(_merge_kernel, gate_row=2),
        grid=(b, s // tm),
        in_specs=[ospec, ospec, ospec, ospec,
                  pl.BlockSpec((1, tm, N_BRANCH * d), lambda bi, i: (bi, i, 0)),
                  pl.BlockSpec((N_BRANCH, BRANCH_W, d), lambda bi, i: (0, 0, 0)),
                  pl.BlockSpec((d, d), lambda bi, i: (0, 0)),
                  pl.BlockSpec((1, tm, d), lambda bi, i: (bi, i, 0)),
                  pl.BlockSpec((1, 6, d), lambda bi, i: (bi, 0, 0)),
                  pl.BlockSpec((1, d), lambda bi, i: (0, 0))],
        out_specs=pl.BlockSpec((1, tm, d), lambda bi, i: (bi, i, 0)),
        out_shape=jax.ShapeDtypeStruct((b, s, d), F32),
        compiler_params=_cparams(("arbitrary", "arbitrary")),
        name="merge_out",
    )(oa, ob, oc, od, gates, w_branch, w_out, x, mod6, post_g.reshape(1, d))


FFN_HALO = 16


def _ffn_kernel(x_ref, halo_ref, mod_ref, pre_g_ref, wg_ref, wv_ref, cwg_ref, cwv_ref, cbg_ref, cbv_ref,
                wd_ref, post_g_ref, o_ref, h_sc, acc_sc, *, tm):
    i = pl.program_id(1)
    c = pl.program_id(2)
    nc = pl.num_programs(2)
    mod = mod_ref[0]

    def modulated(xv):
        ms = jnp.mean(xv * xv, axis=-1, keepdims=True)
        y = xv * lax.rsqrt(ms + EPS) * pre_g_ref[...]
        return y * (1.0 + mod[4:5]) + mod[3:4]

    @pl.when(c == 0)
    def _():
        hh = modulated(halo_ref[0])
        h_sc[0:FFN_HALO, :] = jnp.where(i == 0, 0.0, hh).astype(BF16)
        h_sc[FFN_HALO:FFN_HALO + tm, :] = modulated(x_ref[0]).astype(BF16)
        acc_sc[...] = jnp.zeros(acc_sc.shape, F32)

    hext = h_sc[...]

    def conv(u, cw_ref, cb_ref):
        cw = cw_ref[...]
        out = cb_ref[...] + cw[2:3] * u[FFN_HALO:FFN_HALO + tm]
        out = out + cw[1:2] * u[FFN_HALO - 1:FFN_HALO - 1 + tm]
        return out + cw[0:1] * u[FFN_HALO - 2:FFN_HALO - 2 + tm]

    gate = conv(_nn(hext, wg_ref[...]), cwg_ref, cbg_ref)
    val = conv(_nn(hext, wv_ref[...]), cwv_ref, cbv_ref)
    act = (gate * jax.nn.sigmoid(gate) * val).astype(BF16)
    acc_sc[...] = acc_sc[...] + _nn(act, wd_ref[...])

    @pl.when(c == nc - 1)
    def _():
        y = acc_sc[...]
        yn = y * lax.rsqrt(jnp.mean(y * y, axis=-1, keepdims=True) + EPS) * post_g_ref[...]
        o_ref[0] = x_ref[0] + mod[5:6] * yn


def conv_ffn(x, mod6, pre_g, w_up, conv_w, conv_b, w_down, post_g, tm=512, cf=1408):
    b, s, d = x.shape
    dff = w_down.shape[0]
    tm = min(tm, s)
    nc = dff // cf
    r = tm // FFN_HALO
    cb2 = conv_b.reshape(1, 2 * dff)
    return pl.pallas_call(
        functools.partial(_ffn_kernel, tm=tm),
        grid=(b, s // tm, nc),
        in_specs=[pl.BlockSpec((1, tm, d), lambda bi, i, c: (bi, i, 0)),
                  pl.BlockSpec((1, FFN_HALO, d), lambda bi, i, c: (bi, jnp.maximum(i * r - 1, 0), 0)),
                  pl.BlockSpec((1, 6, d), lambda bi, i, c: (bi, 0, 0)),
                  pl.BlockSpec((1, d), lambda bi, i, c: (0, 0)),
                  pl.BlockSpec((d, cf), lambda bi, i, c: (0, c)),
                  pl.BlockSpec((d, cf), lambda bi, i, c: (0, nc + c)),
                  pl.BlockSpec((FFN_CONV_K, cf), lambda bi, i, c: (0, c)),
                  pl.BlockSpec((FFN_CONV_K, cf), lambda bi, i, c: (0, nc + c)),
                  pl.BlockSpec((1, cf), lambda bi, i, c: (0, c)),
                  pl.BlockSpec((1, cf), lambda bi, i, c: (0, nc + c)),
                  pl.BlockSpec((cf, d), lambda bi, i, c: (c, 0)),
                  pl.BlockSpec((1, d), lambda bi, i, c: (0, 0))],
        out_specs=pl.BlockSpec((1, tm, d), lambda bi, i, c: (bi, i, 0)),
        out_shape=jax.ShapeDtypeStruct((b, s, d), F32),
        scratch_shapes=[pltpu.VMEM((tm + FFN_HALO, d), BF16), pltpu.VMEM((tm, d), F32)],
        compiler_params=_cparams(("arbitrary", "arbitrary", "arbitrary")),
        name="conv_ffn",
    )(x, x, mod6, pre_g.reshape(1, d), w_up, w_up, conv_w, conv_w, cb2, cb2, w_down, post_g.reshape(1, d))


def _reorder_w_in(w_in_l):
    sizes = (512, 256, 256, 256, 256, 256, 256, 256, 64, 64, 256, 32, 8, GATES_W)
    offs = [0]
    for sz in sizes:
        offs.append(offs[-1] + sz)
    col = lambda idx: w_in_l[:, offs[idx]:offs[idx + 1]]
    d = w_in_l.shape[0]
    misc = jnp.concatenate([col(8), col(9), col(11), col(12), jnp.zeros((d, 256 - 168), w_in_l.dtype)], axis=1)
    main = jnp.concatenate([col(0), col(1), col(2), col(3), col(4), col(5), col(6), col(7), misc, col(10)],
                           axis=1)
    return main.astype(BF16), col(13).astype(BF16)


def kernel(x, c, positions, ada_w, ada_b, mix_pre_g, mix_post_g, ffn_pre_g, ffn_post_g, w_in, conv_a_w,
           conv_a_b, conv_a_ln_g, conv_a_ln_b, lam_q1, lam_k1, lam_q2, lam_k2, diff_subln_g, w_branch,
           w_out, w_up, ffn_conv_w, ffn_conv_b, w_down):
    depth = ada_w.shape[0]
    b, s, d = x.shape
    t = min(ATT_T, s)
    mod_all = adaln_mod(c, ada_w, ada_b)
    tabs = rope_tables(positions)
    for l in range(depth):
        lam_init = 0.8 - 0.6 * math.exp(-0.3 * l)
        mod6 = mod_all[l].reshape(b, 6, d)
        w_main, w_gates = _reorder_w_in(w_in[l])
        ua, qbT, kb, vbT, qcT, kc, vcT, qdT, qiT, kdki, vdT, wiT = in_proj(x, mod6, mix_pre_g[l], w_main, tabs, t)
        gates = norm_mod_matmul(x, mod6, mix_pre_g[l], w_gates, BF16, 0, 1, name="in_proj_gates")
        oa = conformer(ua, conv_a_w[l], conv_a_b[l], conv_a_ln_g[l], conv_a_ln_b[l])
        ob = diff_attention(qbT, kb, vbT, lam_q1[l], lam_k1[l], lam_q2[l], lam_k2[l], diff_subln_g[l],
                            lam_init, t)
        oc = sb_attention(qcT, kc, vcT, t)
        od = dsa_attention(qdT, qiT, wiT, kdki, vdT, t)
        x = merge_out(oa, ob, oc, od, gates, w_branch[l].astype(BF16), w_out[l].astype(BF16), x, mod6,
                      mix_post_g[l])
        x = conv_ffn(x, mod6, ffn_pre_g[l], w_up[l].astype(BF16), ffn_conv_w[l], ffn_conv_b[l],
                     w_down[l].astype(BF16), ffn_post_g[l])
    return x
```

```python
import functools
import math

import jax
import jax.numpy as jnp
from jax import lax
from jax.experimental import pallas as pl
from jax.experimental.pallas import tpu as pltpu

F32 = jnp.float32
BF16 = jnp.bfloat16

N_BRANCH = 4
BRANCH_W = 256
ROPE_THETA = 10000.0
EPS = 1e-6
CONV_CH = 256
CONV_K = 31
DIFF_VD = 64
DIFF_QK = 32
DIFF_HEADS = 4
SB_HD = 64
SB_HEADS = 4
DSA_HD = 64
DSA_HEADS = 4
IDX_HEADS = 8
IDX_HD = 32
TOPK_MAX = 256
FFN_CONV_K = 3

ZMAIN_W = 2816
GATES_W = N_BRANCH * 1024
MISC_KD, MISC_VD, MISC_KI, MISC_WI = 0, 64, 128, 160

VMEM_LIMIT = 56 * 1024 * 1024
INT_MIN = -(2 ** 31)
I16_MIN, I16_MAX = -(2 ** 15), 2 ** 15 - 1
NEG_BIG = -1e30
LOG2E = math.log2(math.e)
ATT_T = 512
ONES_ROWS = 16


def _cparams(sem):
    return pltpu.CompilerParams(dimension_semantics=sem, vmem_limit_bytes=VMEM_LIMIT)


def _nn(a, b):
    return jnp.dot(a, b, preferred_element_type=F32)


def _mod_kernel(c_ref, w_ref, b_ref, o_ref):
    c = c_ref[...]
    c_act = c * jax.nn.sigmoid(c)
    o_ref[0] = _nn(c_act, w_ref[0]) + b_ref[0]


def adaln_mod(c, ada_w, ada_b):
    depth, d, n = ada_w.shape
    b = c.shape[0]
    tn = 1024
    return pl.pallas_call(
        _mod_kernel,
        grid=(depth, n // tn),
        in_specs=[pl.BlockSpec((b, d), lambda l, j: (0, 0)),
                  pl.BlockSpec((1, d, tn), lambda l, j: (l, 0, j)),
                  pl.BlockSpec((1, 1, tn), lambda l, j: (l, 0, j))],
        out_specs=pl.BlockSpec((1, b, tn), lambda l, j: (l, 0, j)),
        out_shape=jax.ShapeDtypeStruct((depth, b, n), F32),
        compiler_params=_cparams(("arbitrary", "arbitrary")),
        name="adaln_mod",
    )(c, ada_w, ada_b.reshape(depth, 1, n))


def _rope_table_kernel(pos_ref, f_ref, c32_ref, s32_ref, c64_ref, s64_ref):
    pos = pos_ref[0].astype(F32)
    f = f_ref[...]
    a32 = pos * f[0:1]
    a64 = pos * f[2:3]
    c32_ref[0] = jnp.cos(a32)
    s32_ref[0] = jnp.sin(a32) * f[1:2]
    c64_ref[0] = jnp.cos(a64)
    s64_ref[0] = jnp.sin(a64) * f[3:4]


def rope_tables(positions, ts=512):
    b, s = positions.shape
    ts = min(ts, s)

    def pattern(d):
        inv = ROPE_THETA ** (-jnp.arange(0, d, 2, dtype=F32) / d)
        freq = jnp.tile(jnp.concatenate([inv, inv]), 128 // d)
        sign = jnp.tile(jnp.concatenate([-jnp.ones(d // 2, F32), jnp.ones(d // 2, F32)]), 128 // d)
        return freq, sign

    f32_, s32_ = pattern(IDX_HD)
    f64_, s64_ = pattern(DSA_HD)
    ftab = jnp.stack([f32_, s32_, f64_, s64_])
    tab = jax.ShapeDtypeStruct((b, s, 128), F32)
    spec = pl.BlockSpec((1, ts, 128), lambda bi, i: (bi, i, 0))
    return pl.pallas_call(
        _rope_table_kernel,
        grid=(b, s // ts),
        in_specs=[pl.BlockSpec((1, ts, 1), lambda bi, i: (bi, i, 0)),
                  pl.BlockSpec((4, 128), lambda bi, i: (0, 0))],
        out_specs=[spec] * 4,
        out_shape=[tab] * 4,
        compiler_params=_cparams(("arbitrary", "arbitrary")),
        name="rope_tables",
    )(positions.reshape(b, s, 1), ftab)


def _rope(x, cos, sin_signed, half):
    w = x.shape[1]
    lane = lax.broadcasted_iota(jnp.int32, x.shape, 1)
    first = (lane % (2 * half)) < half
    swapped = jnp.where(first, pltpu.roll(x, w - half, 1), pltpu.roll(x, half, 1))
    return x * cos + swapped * sin_signed


def _inproj_kernel(x_ref, mod_ref, g_ref, w_ref, c32_ref, s32_ref, c64_ref, s64_ref,
                   oua, oqbT, okb, ovbT, oqcT, okc, ovcT, oqdT, oqiT, okdki, ovdT, owiT):
    x = x_ref[0]
    mod = mod_ref[0]
    y = x * lax.rsqrt(jnp.mean(x * x, axis=-1, keepdims=True) + EPS) * g_ref[...]
    h = y * (1.0 + mod[1:2]) + mod[0:1]
    z = _nn(h.astype(BF16), w_ref[...])
    col = lambda cb: z[:, cb * 256:(cb + 1) * 256]
    oua[0] = z[:, 0:2 * CONV_CH]
    c32 = jnp.concatenate([c32_ref[0]] * 2, axis=1)
    s32 = jnp.concatenate([s32_ref[0]] * 2, axis=1)
    c64 = jnp.concatenate([c64_ref[0]] * 2, axis=1)
    s64 = jnp.concatenate([s64_ref[0]] * 2, axis=1)
    h32, h64 = IDX_HD // 2, DSA_HD // 2
    oqbT[0] = (_rope(col(2), c32, s32, h32) * (DIFF_QK ** -0.5 * LOG2E)).T.astype(BF16)
    okb[0] = _rope(col(3), c32, s32, h32).astype(BF16)
    ovbT[0] = col(4).T.astype(BF16)
    oqcT[0] = (col(5) * (SB_HD ** -0.5 * LOG2E)).T.astype(BF16)
    okc[0] = col(6).astype(BF16)
    ovcT[0] = col(7).T.astype(BF16)
    oqdT[0] = (_rope(col(8), c64, s64, h64) * (DSA_HD ** -0.5 * LOG2E)).T.astype(BF16)
    oqiT[0] = _rope(col(10), c32, s32, h32).T.astype(BF16)
    md = col(9)
    md64 = _rope(md, c64, s64, h64)
    md32 = _rope(md, c32, s32, h32)
    ts = md.shape[0]
    okdki[0] = jnp.concatenate(
        [md64[:, MISC_KD:MISC_KD + DSA_HD], md32[:, MISC_KI:MISC_KI + IDX_HD],
         jnp.zeros((ts, 128 - DSA_HD - IDX_HD), F32)], axis=1).astype(BF16)
    mdT = md.T
    ovdT[0, 0] = mdT[MISC_VD:MISC_VD + DSA_HD].astype(BF16)
    owiT[0] = mdT[MISC_WI:MISC_WI + IDX_HEADS] * (IDX_HEADS ** -0.5 * IDX_HD ** -0.5)


def in_proj(x, mod6, g, w_main, tabs, ts):
    b, s, d = x.shape
    tspec = pl.BlockSpec((1, ts, 128), lambda bi, i: (bi, i, 0))
    row = pl.BlockSpec((1, ts, 256), lambda bi, i: (bi, i, 0))
    colT = pl.BlockSpec((1, 256, ts), lambda bi, i: (bi, 0, i))
    s_row = jax.ShapeDtypeStruct((b, s, 256), BF16)
    s_colT = jax.ShapeDtypeStruct((b, 256, s), BF16)
    out_specs = [pl.BlockSpec((1, ts, 2 * CONV_CH), lambda bi, i: (bi, i, 0)),
                 colT, row, colT, colT, row, colT, colT, colT,
                 pl.BlockSpec((1, ts, 128), lambda bi, i: (bi, i, 0)),
                 pl.BlockSpec((1, 1, DSA_HD, ts), lambda bi, i: (bi, i, 0, 0)),
                 pl.BlockSpec((1, IDX_HEADS, ts), lambda bi, i: (bi, 0, i))]
    out_shape = [jax.ShapeDtypeStruct((b, s, 2 * CONV_CH), F32),
                 s_colT, s_row, s_colT, s_colT, s_row, s_colT, s_colT, s_colT,
                 jax.ShapeDtypeStruct((b, s, 128), BF16),
                 jax.ShapeDtypeStruct((b, s // ts, DSA_HD, ts), BF16),
                 jax.ShapeDtypeStruct((b, IDX_HEADS, s), F32)]
    return pl.pallas_call(
        _inproj_kernel,
        grid=(b, s // ts),
        in_specs=[pl.BlockSpec((1, ts, d), lambda bi, i: (bi, i, 0)),
                  pl.BlockSpec((1, 6, d), lambda bi, i: (bi, 0, 0)),
                  pl.BlockSpec((1, d), lambda bi, i: (0, 0)),
                  pl.BlockSpec((d, ZMAIN_W), lambda bi, i: (0, 0))] + [tspec] * 4,
        out_specs=out_specs,
        out_shape=out_shape,
        compiler_params=_cparams(("arbitrary", "arbitrary")),
        name="in_proj",
    )(x, mod6, g.reshape(1, d), w_main, *tabs)


def _masked_rows(xT, lo, hi):
    r = lax.broadcasted_iota(jnp.int32, xT.shape, 0)
    return jnp.where(jnp.logical_and(r >= lo, r < hi), xT, jnp.zeros_like(xT))


def _causal_pairs(n, descending=False):
    pairs = [(i, j) for i in range(n) for j in (range(i, -1, -1) if descending else range(i + 1))]
    return (jnp.array([p[0] for p in pairs], jnp.int32), jnp.array([p[1] for p in pairs], jnp.int32))


CONV_HALO = 32


def _conformer_kernel(u_ref, halo_ref, w_ref, b_ref, g_ref, beta_ref, o_ref, buf, *, ts):
    i = pl.program_id(1)
    u = u_ref[0]
    buf[CONV_HALO:CONV_HALO + ts, :] = u[:, :CONV_CH] * jax.nn.sigmoid(u[:, CONV_CH:])
    uh = halo_ref[0]
    hh = uh[:, :CONV_CH] * jax.nn.sigmoid(uh[:, CONV_CH:])
    buf[0:CONV_HALO, :] = jnp.where(i == 0, 0.0, hh)
    w = w_ref[...]
    acc = jnp.zeros((ts, CONV_CH), F32) + b_ref[...]
    for k in range(CONV_K):
        acc = acc + w[k:k + 1] * buf[pl.ds(CONV_HALO - (CONV_K - 1) + k, ts), :]
    mu = jnp.mean(acc, axis=-1, keepdims=True)
    xc = acc - mu
    y = xc * lax.rsqrt(jnp.mean(xc * xc, axis=-1, keepdims=True) + EPS) * g_ref[...] + beta_ref[...]
    o_ref[0] = (y * jax.nn.sigmoid(y)).astype(o_ref.dtype)


def conformer(ua, conv_w, conv_b, ln_g, ln_b, ts=512):
    b, s, _ = ua.shape
    ts = min(ts, s)
    r = ts // CONV_HALO
    wpad = jnp.concatenate([conv_w, jnp.zeros((32 - CONV_K, CONV_CH), F32)], axis=0)
    vec = lambda a: a.reshape(1, CONV_CH)
    vspec = pl.BlockSpec((1, CONV_CH), lambda bi, i: (0, 0))
    return pl.pallas_call(
        functools.partial(_conformer_kernel, ts=ts),
        grid=(b, s // ts),
        in_specs=[pl.BlockSpec((1, ts, 2 * CONV_CH), lambda bi, i: (bi, i, 0)),
                  pl.BlockSpec((1, CONV_HALO, 2 * CONV_CH), lambda bi, i: (bi, jnp.maximum(i * r - 1, 0), 0)),
                  pl.BlockSpec((32, CONV_CH), lambda bi, i: (0, 0)),
                  vspec, vspec, vspec],
        out_specs=pl.BlockSpec((1, ts, CONV_CH), lambda bi, i: (bi, i, 0)),
        out_shape=jax.ShapeDtypeStruct((b, s, CONV_CH), BF16),
        scratch_shapes=[pltpu.VMEM((ts + CONV_HALO, CONV_CH), F32)],
        compiler_params=_cparams(("arbitrary", "arbitrary")),
        name="conformer",
    )(ua, ua, wpad, vec(conv_b), vec(ln_g), vec(ln_b))


def _diff_kernel(qi_tab, kj_tab, qT_ref, k_ref, vT_ref, lq1, lk1, lq2, lk2, sg_ref, o_ref, qz_sc, m_sc, acc_sc,
                 *, t, lam_init):
    i = qi_tab[pl.program_id(1)]
    j = kj_tab[pl.program_id(1)]
    nmaps = 2 * DIFF_HEADS

    @pl.when(j == 0)
    def _():
        m_sc[...] = jnp.full(m_sc.shape, -jnp.inf, F32)
        acc_sc[...] = jnp.zeros(acc_sc.shape, F32)
        qT = qT_ref[0].astype(F32)
        for jj in range(nmaps):
            g, r = divmod(jj * DIFF_QK, 128)
            qz_sc[jj] = _masked_rows(qT[g * 128:(g + 1) * 128], r, r + DIFF_QK).astype(BF16)

    def step(masked):
        k = k_ref[0]
        vT = vT_ref[0]
        ones = jnp.ones((ONES_ROWS, t), BF16)
        v_ones = [jnp.concatenate([vT[h * DIFF_VD:(h + 1) * DIFF_VD], ones], axis=0)
                  for h in range(DIFF_HEADS)]
        if masked:
            key_i = lax.broadcasted_iota(jnp.int32, (t, t), 0)
            qry_i = lax.broadcasted_iota(jnp.int32, (t, t), 1)
            causal = key_i <= qry_i
        def logits(jj):
            g = (jj * DIFF_QK) // 128
            return _nn(k[:, g * 128:(g + 1) * 128], qz_sc[jj])

        ahead = 2
        queue = [logits(jj) for jj in range(ahead)]
        for jj in range(nmaps):
            h = jj // 2
            s = queue.pop(0)
            if jj + ahead < nmaps:
                queue.append(logits(jj + ahead))
            if masked:
                s = jnp.where(causal, s, -jnp.inf)
            m_prev = m_sc[jj]
            m_cur = jnp.maximum(m_prev, jnp.max(s, axis=0, keepdims=True))
            alpha = jnp.exp2(m_prev - m_cur)
            p = jnp.exp2(s - m_cur)
            acc_sc[jj] = alpha * acc_sc[jj] + _nn(v_ones[h], p.astype(BF16))
            m_sc[jj] = m_cur

    @pl.when(j < i)
    def _():
        step(False)

    @pl.when(j == i)
    def _():
        step(True)
        lam = (jnp.exp(jnp.sum(lq1[...].astype(F32) * lk1[...].astype(F32), axis=1, keepdims=True))
               - jnp.exp(jnp.sum(lq2[...].astype(F32) * lk2[...].astype(F32), axis=1, keepdims=True))
               + lam_init)
        outs = []
        for h in range(DIFF_HEADS):
            a0 = acc_sc[2 * h]
            a1 = acc_sc[2 * h + 1]
            o0 = a0[0:DIFF_VD] / a0[DIFF_VD:DIFF_VD + 1]
            o1 = a1[0:DIFF_VD] / a1[DIFF_VD:DIFF_VD + 1]
            o = o0 - lam * o1
            o = o * lax.rsqrt(jnp.mean(o * o, axis=0, keepdims=True) + EPS) * sg_ref[...]
            outs.append(o * (1.0 - lam_init))
        o_ref[0] = jnp.concatenate(outs, axis=0).T.astype(o_ref.dtype)


def diff_attention(qbT, kb, vbT, lq1, lk1, lq2, lk2, subln_g, lam_init, t):
    b, s, w = kb.shape
    n = s // t
    nmaps = 2 * DIFF_HEADS
    qi_tab, kj_tab = _causal_pairs(n)
    qspec = pl.BlockSpec((1, w, t), lambda bi, p, qi, kj: (bi, 0, qi[p]))
    kspec = pl.BlockSpec((1, t, w), lambda bi, p, qi, kj: (bi, kj[p], 0))
    vspec = pl.BlockSpec((1, w, t), lambda bi, p, qi, kj: (bi, 0, kj[p]))
    pspec = pl.BlockSpec((1, DIFF_QK), lambda bi, p, qi, kj: (0, 0))
    return pl.pallas_call(
        functools.partial(_diff_kernel, t=t, lam_init=lam_init),
        grid_spec=pltpu.PrefetchScalarGridSpec(
            num_scalar_prefetch=2,
            grid=(b, qi_tab.shape[0]),
            in_specs=[qspec, kspec, vspec, pspec, pspec, pspec, pspec,
                      pl.BlockSpec((DIFF_VD, 1), lambda bi, p, qi, kj: (0, 0))],
            out_specs=pl.BlockSpec((1, t, w), lambda bi, p, qi, kj: (bi, qi[p], 0)),
            scratch_shapes=[pltpu.VMEM((nmaps, 128, t), BF16),
                            pltpu.VMEM((nmaps, 1, t), F32),
                            pltpu.VMEM((nmaps, DIFF_VD + ONES_ROWS, t), F32)]),
        out_shape=jax.ShapeDtypeStruct((b, s, w), BF16),
        compiler_params=_cparams(("arbitrary", "arbitrary")),
        name="diff_attention",
    )(qi_tab, kj_tab, qbT, kb, vbT, lq1.reshape(1, -1), lk1.reshape(1, -1), lq2.reshape(1, -1),
      lk2.reshape(1, -1), subln_g.reshape(-1, 1))


SB_BLK = 128


def _sb_kernel(qi_tab, kj_tab, qT_ref, k_ref, vT_ref, o_ref, qz_sc, c_sc, acc_sc, *, t):
    i = qi_tab[pl.program_id(1)]
    j = i - kj_tab[pl.program_id(1)]

    @pl.when(j == 0)
    def _():
        c_sc[...] = jnp.zeros(c_sc.shape, F32)
        acc_sc[...] = jnp.zeros(acc_sc.shape, F32)
        qT = qT_ref[0].astype(F32)
        for h in range(SB_HEADS):
            g, r = divmod(h * SB_HD, 128)
            qz_sc[h] = _masked_rows(qT[g * 128:(g + 1) * 128], r, r + SB_HD).astype(BF16)

    def step(masked):
        k = k_ref[0]
        vT = vT_ref[0]
        ra = lax.broadcasted_iota(jnp.int32, (SB_BLK, SB_BLK), 0)
        rb = lax.broadcasted_iota(jnp.int32, (SB_BLK, SB_BLK), 1)
        upper = jnp.where(rb > ra, 1.0, 0.0).astype(BF16)
        upper2 = jnp.concatenate([upper, upper], axis=1)
        if masked:
            key_i = lax.broadcasted_iota(jnp.int32, (t, t), 0)
            qry_i = lax.broadcasted_iota(jnp.int32, (t, t), 1)
            strict = key_i < qry_i
        def logits(h):
            g = (h * SB_HD) // 128
            return _nn(k[:, g * 128:(g + 1) * 128], qz_sc[h])

        z_next = logits(0)
        for h in range(SB_HEADS):
            z = z_next
            if h + 1 < SB_HEADS:
                z_next = logits(h + 1)
            neg_abs = lax.bitcast_convert_type(
                lax.bitcast_convert_type(z, jnp.uint32) | jnp.uint32(0x80000000), F32)
            ls_pos = jnp.minimum(z, 0.0) - jnp.log(1.0 + jnp.exp2(neg_abs)) * LOG2E
            keep = ls_pos - z
            if masked:
                keep = jnp.where(strict, keep, 0.0)
            hi = keep.astype(BF16)
            lo = (keep - hi.astype(F32)).astype(BF16)
            carry = c_sc[h]
            parts = []
            for blk in range(t // SB_BLK - 1, -1, -1):
                sl = slice(blk * SB_BLK, (blk + 1) * SB_BLK)
                suffix = _nn(upper2, jnp.concatenate([hi[sl], lo[sl]], axis=0))
                parts.append(suffix + carry)
                carry = carry + suffix[0:1] + keep[blk * SB_BLK:blk * SB_BLK + 1]
            c_sc[h] = carry
            later = jnp.concatenate(parts[::-1], axis=0)
            wgt = jnp.exp2(ls_pos + later)
            if masked:
                wgt = jnp.where(strict, wgt, 0.0)
            acc_sc[h] = acc_sc[h] + _nn(vT[h * SB_HD:(h + 1) * SB_HD], wgt.astype(BF16))

    @pl.when(j == 0)
    def _():
        step(True)

    @pl.when(jnp.logical_and(j > 0, j <= i))
    def _():
        step(False)

    @pl.when(j == i)
    def _():
        o_ref[0] = jnp.concatenate([acc_sc[h] for h in range(SB_HEADS)], axis=0).T.astype(o_ref.dtype)


def sb_attention(qcT, kc, vcT, t):
    b, s, w = kc.shape
    n = s // t
    qi_tab, kj_tab = _causal_pairs(n, descending=True)
    qspec = pl.BlockSpec((1, w, t), lambda bi, p, qi, kj: (bi, 0, qi[p]))
    kspec = pl.BlockSpec((1, t, w), lambda bi, p, qi, kj: (bi, kj[p], 0))
    vspec = pl.BlockSpec((1, w, t), lambda bi, p, qi, kj: (bi, 0, kj[p]))
    return pl.pallas_call(
        functools.partial(_sb_kernel, t=t),
        grid_spec=pltpu.PrefetchScalarGridSpec(
            num_scalar_prefetch=2,
            grid=(b, qi_tab.shape[0]),
            in_specs=[qspec, kspec, vspec],
            out_specs=pl.BlockSpec((1, t, w), lambda bi, p, qi, kj: (bi, qi[p], 0)),
            scratch_shapes=[pltpu.VMEM((SB_HEADS, 128, t), BF16),
                            pltpu.VMEM((SB_HEADS, 1, t), F32), pltpu.VMEM((SB_HEADS, SB_HD, t), F32)]),
        out_shape=jax.ShapeDtypeStruct((b, s, w), BF16),
        compiler_params=_cparams(("arbitrary", "arbitrary")),
        name="sb_attention",
    )(qi_tab, kj_tab, qcT, kc, vcT)


def _dsa_kernel(qdT_ref, qiT_ref, wiT_ref, kdki_ref, vdT_ref, o_ref,
                key_sc, half_sc, qz_sc, qiz_sc, tie_sc, m_sc, acc_sc, *, t, topk):
    i = pl.program_id(1)
    ntile = i + 1
    wiT = wiT_ref[0]

    qdT = qdT_ref[0].astype(F32)
    zeros64 = jnp.zeros((128 - DSA_HD, t), F32)
    for h in range(DSA_HEADS):
        qz_sc[h] = jnp.concatenate([qdT[h * DSA_HD:(h + 1) * DSA_HD], zeros64], axis=0).astype(BF16)
    qiT = qiT_ref[0].astype(F32)
    for h in range(IDX_HEADS):
        qiz_sc[h] = jnp.concatenate(
            [jnp.zeros((DSA_HD, t), F32), qiT[h * IDX_HD:(h + 1) * IDX_HD],
             jnp.zeros((128 - DSA_HD - IDX_HD, t), F32)], axis=0).astype(BF16)

    def key_rows(jt):
        return kdki_ref[0, pl.ds(pl.multiple_of(jt * t, t), t), :]

    def score_keys(jt):
        kk = key_rows(jt)
        acc = jnp.zeros((t, t), F32)
        for h in range(IDX_HEADS):
            acc = acc + jnp.maximum(_nn(kk, qiz_sc[h]), 0.0) * wiT[h:h + 1]
        bits = lax.bitcast_convert_type(acc, jnp.int32)
        key = jnp.where(bits < 0, bits ^ jnp.int32(0x7FFFFFFF), bits)
        return jnp.where(acc == 0.0, 0, key)

    def p1(jt, carry):
        key_sc[jt] = score_keys(jt)
        return carry
    lax.fori_loop(0, i, p1, 0)
    key_i = lax.broadcasted_iota(jnp.int32, (t, t), 0)
    qry_i = lax.broadcasted_iota(jnp.int32, (t, t), 1)
    key_sc[i] = jnp.where(key_i <= qry_i, score_keys(i), INT_MIN)

    def count_ge(cand):
        def body(jt, part):
            ge = jnp.where(key_sc[jt] >= cand, 1, 0)
            return part + jnp.sum(ge.reshape(t // 8, 8, t), axis=0)
        part = lax.fori_loop(0, ntile, body, jnp.zeros((8, t), jnp.int32))
        return jnp.sum(part, axis=0, keepdims=True)

    def count16_ge(cand):
        cand16 = cand.astype(jnp.int16)

        def body(jt, acc):
            ge = jnp.where(half_sc[jt] >= cand16, jnp.bfloat16(1), jnp.bfloat16(0))
            g3 = ge.reshape(t // 16, 16, t)
            part = g3[0]
            for r in range(1, t // 16):
                part = part + g3[r]
            return acc + part.astype(F32)
        acc = lax.fori_loop(0, ntile, body, jnp.zeros((16, t), F32))
        return jnp.sum(acc, axis=0, keepdims=True)

    def search16(quota):
        def bis(it, thr):
            cand = thr + lax.shift_left(jnp.int32(1), jnp.int32(15) - it)
            return jnp.where(count16_ge(cand) >= quota, cand, thr)
        return lax.fori_loop(0, 16, bis, jnp.full((1, t), I16_MIN, jnp.int32))

    def fill_hi(jt, carry):
        half_sc[jt] = lax.shift_right_arithmetic(key_sc[jt], 16).astype(jnp.int16)
        return carry
    lax.fori_loop(0, ntile, fill_hi, 0)
    thr_hi = search16(jnp.float32(topk))
    above = jnp.where(thr_hi < I16_MAX, count16_ge(jnp.minimum(thr_hi + 1, I16_MAX)), 0.0)

    def fill_lo(jt, carry):
        key = key_sc[jt]
        low = (key & 0xFFFF) + I16_MIN
        in_group = lax.shift_right_arithmetic(key, 16) == thr_hi
        half_sc[jt] = jnp.where(in_group, low, I16_MIN).astype(jnp.int16)
        return carry
    lax.fori_loop(0, ntile, fill_lo, 0)
    thr_lo = search16(topk - above)
    thr = lax.shift_left(thr_hi, 16) | (thr_lo - I16_MIN)
    thr = jnp.maximum(thr, INT_MIN + 1)
    n_ge = count_ge(thr)

    @pl.when(jnp.max(n_ge) > topk)
    def _():
        n_gt = count_ge(thr + 1)
        quota = (topk - n_gt).astype(F32)
        tie_sc[...] = jnp.zeros(tie_sc.shape, F32)
        lower = jnp.where(qry_i < key_i, 1.0, 0.0).astype(BF16)

        def fix(jt, carry):
            key = key_sc[jt]
            eq = key == thr
            eqf = jnp.where(eq, 1.0, 0.0)
            rank = _nn(lower, eqf.astype(BF16)) + tie_sc[...]
            drop = jnp.logical_and(eq, rank >= quota)
            key_sc[jt] = jnp.where(drop, INT_MIN, key)
            tie_sc[...] = tie_sc[...] + jnp.sum(eqf, axis=0, keepdims=True)
            return carry
        lax.fori_loop(0, ntile, fix, 0)

    m_sc[...] = jnp.full(m_sc.shape, NEG_BIG, F32)
    acc_sc[...] = jnp.zeros(acc_sc.shape, F32)

    def p3(jt, carry):
        sel = key_sc[jt] >= thr
        kk = key_rows(jt)
        vT = jnp.concatenate([vdT_ref[0, jt], jnp.ones((ONES_ROWS, t), BF16)], axis=0)
        s_next = _nn(kk, qz_sc[0])
        for h in range(DSA_HEADS):
            s = jnp.where(sel, s_next, NEG_BIG)
            if h + 1 < DSA_HEADS:
                s_next = _nn(kk, qz_sc[h + 1])
            m_prev = m_sc[h]
            m_cur = jnp.maximum(m_prev, jnp.max(s, axis=0, keepdims=True))
            alpha = jnp.exp2(m_prev - m_cur)
            p = jnp.exp2(s - m_cur)
            acc_sc[h] = alpha * acc_sc[h] + _nn(vT, p.astype(BF16))
            m_sc[h] = m_cur
        return carry
    lax.fori_loop(0, ntile, p3, 0)
    o_ref[0] = jnp.concatenate(
        [acc_sc[h][0:DSA_HD] / acc_sc[h][DSA_HD:DSA_HD + 1] for h in range(DSA_HEADS)],
        axis=0).T.astype(o_ref.dtype)


def dsa_attention(qdT, qiT, wiT, kdki, vdT, t):
    b, w, s = qdT.shape
    n = s // t
    topk = min(TOPK_MAX, s // 4)
    qspec = pl.BlockSpec((1, w, t), lambda bi, i: (bi, 0, i))
    return pl.pallas_call(
        functools.partial(_dsa_kernel, t=t, topk=topk),
        grid=(b, n),
        in_specs=[qspec, qspec, pl.BlockSpec((1, IDX_HEADS, t), lambda bi, i: (bi, 0, i)),
                  pl.BlockSpec((1, s, 128), lambda bi, i: (bi, 0, 0)),
                  pl.BlockSpec((1, n, DSA_HD, t), lambda bi, i: (bi, 0, 0, 0))],
        out_specs=pl.BlockSpec((1, t, w), lambda bi, i: (bi, i, 0)),
        out_shape=jax.ShapeDtypeStruct((b, s, w), BF16),
        scratch_shapes=[pltpu.VMEM((n, t, t), jnp.int32), pltpu.VMEM((n, t, t), jnp.int16),
                        pltpu.VMEM((DSA_HEADS, 128, t), BF16), pltpu.VMEM((IDX_HEADS, 128, t), BF16),
                        pltpu.VMEM((1, t), F32),
                        pltpu.VMEM((DSA_HEADS, 1, t), F32),
                        pltpu.VMEM((DSA_HEADS, DSA_HD + ONES_ROWS, t), F32)],
        compiler_params=_cparams(("arbitrary", "arbitrary")),
        name="dsa_attention",
    )(qdT, qiT, wiT, kdki, vdT)


def _merge_kernel(oa_ref, ob_ref, oc_ref, od_ref, wg_ref, wb_ref, wo_ref, x_ref, mod_ref, pre_g_ref, pg_ref,
                  o_ref):
    d = x_ref.shape[2]
    x = x_ref[0]
    mod = mod_ref[0]
    y0 = x * lax.rsqrt(jnp.mean(x * x, axis=-1, keepdims=True) + EPS) * pre_g_ref[...]
    h = (y0 * (1.0 + mod[1:2]) + mod[0:1]).astype(BF16)

    def gate_logits(bi_):
        return _nn(h, wg_ref[:, bi_ * d:(bi_ + 1) * d])

    merged = None
    nxt = gate_logits(0)
    for bi_, o_r in enumerate((oa_ref, ob_ref, oc_ref, od_ref)):
        logit = nxt
        if bi_ + 1 < N_BRANCH:
            nxt = gate_logits(bi_ + 1)
        term = jax.nn.sigmoid(logit) * _nn(o_r[0], wb_ref[bi_])
        merged = term if merged is None else merged + term
    y = _nn(merged.astype(BF16), wo_ref[...])
    yn = y * lax.rsqrt(jnp.mean(y * y, axis=-1, keepdims=True) + EPS) * pg_ref[...]
    o_ref[0] = x + mod[2:3] * yn


def merge_out(oa, ob, oc, od, w_gates, w_branch, w_out, x, mod6, pre_g, post_g, tm=512):
    b, s, d = x.shape
    tm = min(tm, s)
    ospec = pl.BlockSpec((1, tm, BRANCH_W), lambda bi, i: (bi, i, 0))
    resident = lambda shape: pl.BlockSpec(shape, lambda bi, i: (0,) * len(shape), pipeline_mode=pl.Buffered(1))
    return pl.pallas_call(
        _merge_kernel,
        grid=(b, s // tm),
        in_specs=[ospec, ospec, ospec, ospec,
                  resident((d, N_BRANCH * d)),
                  resident((N_BRANCH, BRANCH_W, d)),
                  resident((d, d)),
                  pl.BlockSpec((1, tm, d), lambda bi, i: (bi, i, 0)),
                  pl.BlockSpec((1, 6, d), lambda bi, i: (bi, 0, 0)),
                  pl.BlockSpec((1, d), lambda bi, i: (0, 0)),
                  pl.BlockSpec((1, d), lambda bi, i: (0, 0))],
        out_specs=pl.BlockSpec((1, tm, d), lambda bi, i: (bi, i, 0)),
        out_shape=jax.ShapeDtypeStruct((b, s, d), F32),
        compiler_params=_cparams(("arbitrary", "arbitrary")),
        name="merge_out",
    )(oa, ob, oc, od, w_gates, w_branch, w_out, x, mod6, pre_g.reshape(1, d), post_g.reshape(1, d))


FFN_HALO = 16


FFN_SUB = 256


def _ffn_kernel(x_ref, halo_ref, mod_ref, pre_g_ref, wup_ref, cw_ref, cb_ref, wd_ref, post_g_ref, o_ref,
                act_sc, *, tm, dff):
    i = pl.program_id(1)
    mod = mod_ref[0]

    def modulated(xv):
        ms = jnp.mean(xv * xv, axis=-1, keepdims=True)
        y = xv * lax.rsqrt(ms + EPS) * pre_g_ref[...]
        return y * (1.0 + mod[4:5]) + mod[3:4]

    x = x_ref[0]
    halo = jnp.where(i == 0, 0.0, modulated(halo_ref[0]))
    hext = jnp.concatenate([halo.astype(BF16), modulated(x).astype(BF16)], axis=0)

    def up(col):
        return _nn(hext, wup_ref[:, col:col + FFN_SUB])

    def conv(u, col):
        cw = cw_ref[:, col:col + FFN_SUB]
        out = cb_ref[:, col:col + FFN_SUB] + cw[2:3] * u[FFN_HALO:FFN_HALO + tm]
        out = out + cw[1:2] * u[FFN_HALO - 1:FFN_HALO - 1 + tm]
        return out + cw[0:1] * u[FFN_HALO - 2:FFN_HALO - 2 + tm]

    nsub = dff // FFN_SUB
    nxt = (up(0), up(dff))
    for sb in range(nsub):
        ug, uv = nxt
        if sb + 1 < nsub:
            nxt = (up((sb + 1) * FFN_SUB), up(dff + (sb + 1) * FFN_SUB))
        gate = conv(ug, sb * FFN_SUB)
        val = conv(uv, dff + sb * FFN_SUB)
        act_sc[:, sb * FFN_SUB:(sb + 1) * FFN_SUB] = (gate * jax.nn.sigmoid(gate) * val).astype(BF16)
    y = _nn(act_sc[...], wd_ref[...])
    yn = y * lax.rsqrt(jnp.mean(y * y, axis=-1, keepdims=True) + EPS) * post_g_ref[...]
    o_ref[0] = x + mod[5:6] * yn


def conv_ffn(x, mod6, pre_g, w_up, conv_w, conv_b, w_down, post_g, tm=512):
    b, s, d = x.shape
    dff = w_down.shape[0]
    tm = min(tm, s)
    r = tm // FFN_HALO
    resident = lambda shape: pl.BlockSpec(shape, lambda bi, i: (0,) * len(shape), pipeline_mode=pl.Buffered(1))
    return pl.pallas_call(
        functools.partial(_ffn_kernel, tm=tm, dff=dff),
        grid=(b, s // tm),
        in_specs=[pl.BlockSpec((1, tm, d), lambda bi, i: (bi, i, 0)),
                  pl.BlockSpec((1, FFN_HALO, d), lambda bi, i: (bi, jnp.maximum(i * r - 1, 0), 0)),
                  pl.BlockSpec((1, 6, d), lambda bi, i: (bi, 0, 0)),
                  pl.BlockSpec((1, d), lambda bi, i: (0, 0)),
                  resident((d, 2 * dff)),
                  pl.BlockSpec((FFN_CONV_K, 2 * dff), lambda bi, i: (0, 0)),
                  pl.BlockSpec((1, 2 * dff), lambda bi, i: (0, 0)),
                  resident((dff, d)),
                  pl.BlockSpec((1, d), lambda bi, i: (0, 0))],
        out_specs=pl.BlockSpec((1, tm, d), lambda bi, i: (bi, i, 0)),
        out_shape=jax.ShapeDtypeStruct((b, s, d), F32),
        scratch_shapes=[pltpu.VMEM((tm, dff), BF16)],
        compiler_params=_cparams(("arbitrary", "arbitrary")),
        name="conv_ffn",
    )(x, x, mod6, pre_g.reshape(1, d), w_up, conv_w, conv_b.reshape(1, 2 * dff), w_down, post_g.reshape(1, d))


def _reorder_w_in(w_in_l):
    sizes = (512, 256, 256, 256, 256, 256, 256, 256, 64, 64, 256, 32, 8, GATES_W)
    offs = [0]
    for sz in sizes:
        offs.append(offs[-1] + sz)
    col = lambda idx: w_in_l[:, offs[idx]:offs[idx + 1]]
    d = w_in_l.shape[0]
    misc = jnp.concatenate([col(8), col(9), col(11), col(12), jnp.zeros((d, 256 - 168), w_in_l.dtype)], axis=1)
    main = jnp.concatenate([col(0), col(1), col(2), col(3), col(4), col(5), col(6), col(7), misc, col(10)],
                           axis=1)
    return main.astype(BF16), col(13).astype(BF16)


def kernel(x, c, positions, ada_w, ada_b, mix_pre_g, mix_post_g, ffn_pre_g, ffn_post_g, w_in, conv_a_w,
           conv_a_b, conv_a_ln_g, conv_a_ln_b, lam_q1, lam_k1, lam_q2, lam_k2, diff_subln_g, w_branch,
           w_out, w_up, ffn_conv_w, ffn_conv_b, w_down):
    depth = ada_w.shape[0]
    b, s, d = x.shape
    t = min(ATT_T, s)
    mod_all = adaln_mod(c, ada_w, ada_b)
    tabs = rope_tables(positions)
    for l in range(depth):
        lam_init = 0.8 - 0.6 * math.exp(-0.3 * l)
        mod6 = mod_all[l].reshape(b, 6, d)
        w_main, w_gates = _reorder_w_in(w_in[l])
        ua, qbT, kb, vbT, qcT, kc, vcT, qdT, qiT, kdki, vdT, wiT = in_proj(x, mod6, mix_pre_g[l], w_main, tabs, t)
        oa = conformer(ua, conv_a_w[l], conv_a_b[l], conv_a_ln_g[l], conv_a_ln_b[l])
        ob = diff_attention(qbT, kb, vbT, lam_q1[l], lam_k1[l], lam_q2[l], lam_k2[l], diff_subln_g[l],
                            lam_init, t)
        oc = sb_attention(qcT, kc, vcT, t)
        od = dsa_attention(qdT, qiT, wiT, kdki, vdT, t)
        x = merge_out(oa, ob, oc, od, w_gates, w_branch[l].astype(BF16), w_out[l].astype(BF16), x, mod6,
                      mix_pre_g[l], mix_post_g[l])
        x = conv_ffn(x, mod6, ffn_pre_g[l], w_up[l].astype(BF16), ffn_conv_w[l], ffn_conv_b[l],
                     w_down[l].astype(BF16), ffn_post_g[l])
    return x
```

```python
import functools
import math

import jax
import jax.numpy as jnp
from jax import lax
from jax.experimental import pallas as pl
from jax.experimental.pallas import tpu as pltpu

F32 = jnp.float32
BF16 = jnp.bfloat16

N_BRANCH = 4
BRANCH_W = 256
ROPE_THETA = 10000.0
EPS = 1e-6
CONV_CH = 256
CONV_K = 31
DIFF_VD = 64
DIFF_QK = 32
DIFF_HEADS = 4
SB_HD = 64
SB_HEADS = 4
DSA_HD = 64
DSA_HEADS = 4
IDX_HEADS = 8
IDX_HD = 32
TOPK_MAX = 256
FFN_CONV_K = 3

ZMAIN_W = 2816
GATES_W = N_BRANCH * 1024
MISC_KD, MISC_VD, MISC_KI, MISC_WI = 0, 64, 128, 160

VMEM_LIMIT = 56 * 1024 * 1024
INT_MIN = -(2 ** 31)
I16_MIN, I16_MAX = -(2 ** 15), 2 ** 15 - 1
NEG_BIG = -1e30
LOG2E = math.log2(math.e)
ATT_T = 512
ONES_ROWS = 16


def _cparams(sem):
    return pltpu.CompilerParams(dimension_semantics=sem, vmem_limit_bytes=VMEM_LIMIT)


def _nn(a, b):
    return jnp.dot(a, b, preferred_element_type=F32)


def _mod_kernel(c_ref, w_ref, b_ref, o_ref):
    c = c_ref[...]
    c_act = c * jax.nn.sigmoid(c)
    o_ref[0] = _nn(c_act, w_ref[0]) + b_ref[0]


def adaln_mod(c, ada_w, ada_b):
    depth, d, n = ada_w.shape
    b = c.shape[0]
    tn = 1024
    return pl.pallas_call(
        _mod_kernel,
        grid=(depth, n // tn),
        in_specs=[pl.BlockSpec((b, d), lambda l, j: (0, 0)),
                  pl.BlockSpec((1, d, tn), lambda l, j: (l, 0, j)),
                  pl.BlockSpec((1, 1, tn), lambda l, j: (l, 0, j))],
        out_specs=pl.BlockSpec((1, b, tn), lambda l, j: (l, 0, j)),
        out_shape=jax.ShapeDtypeStruct((depth, b, n), F32),
        compiler_params=_cparams(("arbitrary", "arbitrary")),
        name="adaln_mod",
    )(c, ada_w, ada_b.reshape(depth, 1, n))


def _rope_table_kernel(pos_ref, f_ref, c32_ref, s32_ref, c64_ref, s64_ref):
    pos = pos_ref[0].astype(F32)
    f = f_ref[...]
    a32 = pos * f[0:1]
    a64 = pos * f[2:3]
    c32_ref[0] = jnp.cos(a32)
    s32_ref[0] = jnp.sin(a32) * f[1:2]
    c64_ref[0] = jnp.cos(a64)
    s64_ref[0] = jnp.sin(a64) * f[3:4]


def rope_tables(positions, ts=512):
    b, s = positions.shape
    ts = min(ts, s)

    def pattern(d):
        inv = ROPE_THETA ** (-jnp.arange(0, d, 2, dtype=F32) / d)
        freq = jnp.tile(jnp.concatenate([inv, inv]), 128 // d)
        sign = jnp.tile(jnp.concatenate([-jnp.ones(d // 2, F32), jnp.ones(d // 2, F32)]), 128 // d)
        return freq, sign

    f32_, s32_ = pattern(IDX_HD)
    f64_, s64_ = pattern(DSA_HD)
    ftab = jnp.stack([f32_, s32_, f64_, s64_])
    tab = jax.ShapeDtypeStruct((b, s, 128), F32)
    spec = pl.BlockSpec((1, ts, 128), lambda bi, i: (bi, i, 0))
    return pl.pallas_call(
        _rope_table_kernel,
        grid=(b, s // ts),
        in_specs=[pl.BlockSpec((1, ts, 1), lambda bi, i: (bi, i, 0)),
                  pl.BlockSpec((4, 128), lambda bi, i: (0, 0))],
        out_specs=[spec] * 4,
        out_shape=[tab] * 4,
        compiler_params=_cparams(("arbitrary", "arbitrary")),
        name="rope_tables",
    )(positions.reshape(b, s, 1), ftab)


def _rope(x, cos, sin_signed, half):
    w = x.shape[1]
    lane = lax.broadcasted_iota(jnp.int32, x.shape, 1)
    first = (lane % (2 * half)) < half
    swapped = jnp.where(first, pltpu.roll(x, w - half, 1), pltpu.roll(x, half, 1))
    return x * cos + swapped * sin_signed


def _inproj_kernel(x_ref, mod_ref, g_ref, w_ref, c32_ref, s32_ref, c64_ref, s64_ref,
                   oua, oqbT, okb, ovbT, oqcT, okc, ovcT, oqdT, oqiT, okdki, ovdT, owiT):
    x = x_ref[0]
    mod = mod_ref[0]
    y = x * lax.rsqrt(jnp.mean(x * x, axis=-1, keepdims=True) + EPS) * g_ref[...]
    h = y * (1.0 + mod[1:2]) + mod[0:1]
    z = _nn(h.astype(BF16), w_ref[...])
    col = lambda cb: z[:, cb * 256:(cb + 1) * 256]
    oua[0] = z[:, 0:2 * CONV_CH]
    c32 = jnp.concatenate([c32_ref[0]] * 2, axis=1)
    s32 = jnp.concatenate([s32_ref[0]] * 2, axis=1)
    c64 = jnp.concatenate([c64_ref[0]] * 2, axis=1)
    s64 = jnp.concatenate([s64_ref[0]] * 2, axis=1)
    h32, h64 = IDX_HD // 2, DSA_HD // 2
    oqbT[0] = (_rope(col(2), c32, s32, h32) * (DIFF_QK ** -0.5 * LOG2E)).T.astype(BF16)
    okb[0] = _rope(col(3), c32, s32, h32).astype(BF16)
    ovbT[0] = col(4).T.astype(BF16)
    oqcT[0] = (col(5) * (SB_HD ** -0.5 * LOG2E)).T.astype(BF16)
    okc[0] = col(6).astype(BF16)
    ovcT[0] = col(7).T.astype(BF16)
    oqdT[0] = (_rope(col(8), c64, s64, h64) * (DSA_HD ** -0.5 * LOG2E)).T.astype(BF16)
    oqiT[0] = _rope(col(10), c32, s32, h32).T.astype(BF16)
    md = col(9)
    md64 = _rope(md, c64, s64, h64)
    md32 = _rope(md, c32, s32, h32)
    ts = md.shape[0]
    okdki[0] = jnp.concatenate(
        [md64[:, MISC_KD:MISC_KD + DSA_HD], md32[:, MISC_KI:MISC_KI + IDX_HD],
         jnp.zeros((ts, 128 - DSA_HD - IDX_HD), F32)], axis=1).astype(BF16)
    mdT = md.T
    ovdT[0, 0] = mdT[MISC_VD:MISC_VD + DSA_HD].astype(BF16)
    owiT[0] = mdT[MISC_WI:MISC_WI + IDX_HEADS] * (IDX_HEADS ** -0.5 * IDX_HD ** -0.5)


def in_proj(x, mod6, g, w_main, tabs, ts):
    b, s, d = x.shape
    tspec = pl.BlockSpec((1, ts, 128), lambda bi, i: (bi, i, 0))
    row = pl.BlockSpec((1, ts, 256), lambda bi, i: (bi, i, 0))
    colT = pl.BlockSpec((1, 256, ts), lambda bi, i: (bi, 0, i))
    s_row = jax.ShapeDtypeStruct((b, s, 256), BF16)
    s_colT = jax.ShapeDtypeStruct((b, 256, s), BF16)
    out_specs = [pl.BlockSpec((1, ts, 2 * CONV_CH), lambda bi, i: (bi, i, 0)),
                 colT, row, colT, colT, row, colT, colT, colT,
                 pl.BlockSpec((1, ts, 128), lambda bi, i: (bi, i, 0)),
                 pl.BlockSpec((1, 1, DSA_HD, ts), lambda bi, i: (bi, i, 0, 0)),
                 pl.BlockSpec((1, IDX_HEADS, ts), lambda bi, i: (bi, 0, i))]
    out_shape = [jax.ShapeDtypeStruct((b, s, 2 * CONV_CH), F32),
                 s_colT, s_row, s_colT, s_colT, s_row, s_colT, s_colT, s_colT,
                 jax.ShapeDtypeStruct((b, s, 128), BF16),
                 jax.ShapeDtypeStruct((b, s // ts, DSA_HD, ts), BF16),
                 jax.ShapeDtypeStruct((b, IDX_HEADS, s), F32)]
    return pl.pallas_call(
        _inproj_kernel,
        grid=(b, s // ts),
        in_specs=[pl.BlockSpec((1, ts, d), lambda bi, i: (bi, i, 0)),
                  pl.BlockSpec((1, 6, d), lambda bi, i: (bi, 0, 0)),
                  pl.BlockSpec((1, d), lambda bi, i: (0, 0)),
                  pl.BlockSpec((d, ZMAIN_W), lambda bi, i: (0, 0))] + [tspec] * 4,
        out_specs=out_specs,
        out_shape=out_shape,
        compiler_params=_cparams(("arbitrary", "arbitrary")),
        name="in_proj",
    )(x, mod6, g.reshape(1, d), w_main, *tabs)


def _masked_rows(xT, lo, hi):
    r = lax.broadcasted_iota(jnp.int32, xT.shape, 0)
    return jnp.where(jnp.logical_and(r >= lo, r < hi), xT, jnp.zeros_like(xT))


def _causal_pairs(n, descending=False):
    pairs = [(i, j) for i in range(n) for j in (range(i, -1, -1) if descending else range(i + 1))]
    return (jnp.array([p[0] for p in pairs], jnp.int32), jnp.array([p[1] for p in pairs], jnp.int32))


CONV_HALO = 32


def _conformer_kernel(u_ref, halo_ref, w_ref, b_ref, g_ref, beta_ref, o_ref, buf, aligned, *, ts):
    i = pl.program_id(1)
    u = u_ref[0]
    buf[CONV_HALO:CONV_HALO + ts, :] = u[:, :CONV_CH] * jax.nn.sigmoid(u[:, CONV_CH:])
    uh = halo_ref[0]
    hh = uh[:, :CONV_CH] * jax.nn.sigmoid(uh[:, CONV_CH:])
    buf[0:CONV_HALO, :] = jnp.where(i == 0, 0.0, hh)
    w = w_ref[...]
    acc = jnp.zeros((ts, CONV_CH), F32) + b_ref[...]
    first = CONV_HALO - (CONV_K - 1)
    for r in range(8):
        taps = [k for k in range(CONV_K) if (first + k) % 8 == r]
        span = max(first + k for k in taps) - r
        aligned[0:ts + span, :] = buf[pl.ds(r, ts + span), :]
        for k in taps:
            o = first + k - r
            acc = acc + w[k:k + 1] * aligned[o:o + ts, :]
    mu = jnp.mean(acc, axis=-1, keepdims=True)
    xc = acc - mu
    y = xc * lax.rsqrt(jnp.mean(xc * xc, axis=-1, keepdims=True) + EPS) * g_ref[...] + beta_ref[...]
    o_ref[0] = (y * jax.nn.sigmoid(y)).astype(o_ref.dtype)


def conformer(ua, conv_w, conv_b, ln_g, ln_b, ts=512):
    b, s, _ = ua.shape
    ts = min(ts, s)
    r = ts // CONV_HALO
    wpad = jnp.concatenate([conv_w, jnp.zeros((32 - CONV_K, CONV_CH), F32)], axis=0)
    vec = lambda a: a.reshape(1, CONV_CH)
    vspec = pl.BlockSpec((1, CONV_CH), lambda bi, i: (0, 0))
    return pl.pallas_call(
        functools.partial(_conformer_kernel, ts=ts),
        grid=(b, s // ts),
        in_specs=[pl.BlockSpec((1, ts, 2 * CONV_CH), lambda bi, i: (bi, i, 0)),
                  pl.BlockSpec((1, CONV_HALO, 2 * CONV_CH), lambda bi, i: (bi, jnp.maximum(i * r - 1, 0), 0)),
                  pl.BlockSpec((32, CONV_CH), lambda bi, i: (0, 0)),
                  vspec, vspec, vspec],
        out_specs=pl.BlockSpec((1, ts, CONV_CH), lambda bi, i: (bi, i, 0)),
        out_shape=jax.ShapeDtypeStruct((b, s, CONV_CH), BF16),
        scratch_shapes=[pltpu.VMEM((ts + CONV_HALO, CONV_CH), F32),
                        pltpu.VMEM((ts + CONV_HALO, CONV_CH), F32)],
        compiler_params=_cparams(("arbitrary", "arbitrary")),
        name="conformer",
    )(ua, ua, wpad, vec(conv_b), vec(ln_g), vec(ln_b))


def _diff_kernel(qi_tab, kj_tab, qT_ref, k_ref, vT_ref, lq1, lk1, lq2, lk2, sg_ref, o_ref, qz_sc, m_sc, acc_sc,
                 *, t, lam_init):
    i = qi_tab[pl.program_id(1)]
    j = kj_tab[pl.program_id(1)]
    nmaps = 2 * DIFF_HEADS

    @pl.when(j == 0)
    def _():
        m_sc[...] = jnp.full(m_sc.shape, -jnp.inf, F32)
        acc_sc[...] = jnp.zeros(acc_sc.shape, F32)
        qT = qT_ref[0].astype(F32)
        for jj in range(nmaps):
            g, r = divmod(jj * DIFF_QK, 128)
            qz_sc[jj] = _masked_rows(qT[g * 128:(g + 1) * 128], r, r + DIFF_QK).astype(BF16)

    def step(masked):
        k = k_ref[0]
        vT = vT_ref[0]
        ones = jnp.ones((ONES_ROWS, t), BF16)
        v_ones = [jnp.concatenate([vT[h * DIFF_VD:(h + 1) * DIFF_VD], ones], axis=0)
                  for h in range(DIFF_HEADS)]
        if masked:
            key_i = lax.broadcasted_iota(jnp.int32, (t, t), 0)
            qry_i = lax.broadcasted_iota(jnp.int32, (t, t), 1)
            causal = key_i <= qry_i
        def logits(jj):
            g = (jj * DIFF_QK) // 128
            return _nn(k[:, g * 128:(g + 1) * 128], qz_sc[jj])

        ahead = 2
        queue = [logits(jj) for jj in range(ahead)]
        for jj in range(nmaps):
            h = jj // 2
            s = queue.pop(0)
            if jj + ahead < nmaps:
                queue.append(logits(jj + ahead))
            if masked:
                s = jnp.where(causal, s, -jnp.inf)
            m_prev = m_sc[jj]
            m_cur = jnp.maximum(m_prev, jnp.max(s, axis=0, keepdims=True))
            alpha = jnp.exp2(m_prev - m_cur)
            p = jnp.exp2(s - m_cur)
            acc_sc[jj] = alpha * acc_sc[jj] + _nn(v_ones[h], p.astype(BF16))
            m_sc[jj] = m_cur

    @pl.when(j < i)
    def _():
        step(False)

    @pl.when(j == i)
    def _():
        step(True)
        lam = (jnp.exp(jnp.sum(lq1[...].astype(F32) * lk1[...].astype(F32), axis=1, keepdims=True))
               - jnp.exp(jnp.sum(lq2[...].astype(F32) * lk2[...].astype(F32), axis=1, keepdims=True))
               + lam_init)
        outs = []
        for h in range(DIFF_HEADS):
            a0 = acc_sc[2 * h]
            a1 = acc_sc[2 * h + 1]
            o0 = a0[0:DIFF_VD] / a0[DIFF_VD:DIFF_VD + 1]
            o1 = a1[0:DIFF_VD] / a1[DIFF_VD:DIFF_VD + 1]
            o = o0 - lam * o1
            o = o * lax.rsqrt(jnp.mean(o * o, axis=0, keepdims=True) + EPS) * sg_ref[...]
            outs.append(o * (1.0 - lam_init))
        o_ref[0] = jnp.concatenate(outs, axis=0).T.astype(o_ref.dtype)


def diff_attention(qbT, kb, vbT, lq1, lk1, lq2, lk2, subln_g, lam_init, t):
    b, s, w = kb.shape
    n = s // t
    nmaps = 2 * DIFF_HEADS
    qi_tab, kj_tab = _causal_pairs(n)
    qspec = pl.BlockSpec((1, w, t), lambda bi, p, qi, kj: (bi, 0, qi[p]))
    kspec = pl.BlockSpec((1, t, w), lambda bi, p, qi, kj: (bi, kj[p], 0))
    vspec = pl.BlockSpec((1, w, t), lambda bi, p, qi, kj: (bi, 0, kj[p]))
    pspec = pl.BlockSpec((1, DIFF_QK), lambda bi, p, qi, kj: (0, 0))
    return pl.pallas_call(
        functools.partial(_diff_kernel, t=t, lam_init=lam_init),
        grid_spec=pltpu.PrefetchScalarGridSpec(
            num_scalar_prefetch=2,
            grid=(b, qi_tab.shape[0]),
            in_specs=[qspec, kspec, vspec, pspec, pspec, pspec, pspec,
                      pl.BlockSpec((DIFF_VD, 1), lambda bi, p, qi, kj: (0, 0))],
            out_specs=pl.BlockSpec((1, t, w), lambda bi, p, qi, kj: (bi, qi[p], 0)),
            scratch_shapes=[pltpu.VMEM((nmaps, 128, t), BF16),
                            pltpu.VMEM((nmaps, 1, t), F32),
                            pltpu.VMEM((nmaps, DIFF_VD + ONES_ROWS, t), F32)]),
        out_shape=jax.ShapeDtypeStruct((b, s, w), BF16),
        compiler_params=_cparams(("arbitrary", "arbitrary")),
        name="diff_attention",
    )(qi_tab, kj_tab, qbT, kb, vbT, lq1.reshape(1, -1), lk1.reshape(1, -1), lq2.reshape(1, -1),
      lk2.reshape(1, -1), subln_g.reshape(-1, 1))


SB_BLK = 128


def _sb_kernel(qi_tab, kj_tab, qT_ref, k_ref, vT_ref, o_ref, qz_sc, c_sc, acc_sc, *, t):
    i = qi_tab[pl.program_id(1)]
    j = i - kj_tab[pl.program_id(1)]

    @pl.when(j == 0)
    def _():
        c_sc[...] = jnp.zeros(c_sc.shape, F32)
        acc_sc[...] = jnp.zeros(acc_sc.shape, F32)
        qT = qT_ref[0].astype(F32)
        for h in range(SB_HEADS):
            g, r = divmod(h * SB_HD, 128)
            qz_sc[h] = _masked_rows(qT[g * 128:(g + 1) * 128], r, r + SB_HD).astype(BF16)

    def step(masked):
        k = k_ref[0]
        vT = vT_ref[0]
        ra = lax.broadcasted_iota(jnp.int32, (SB_BLK, SB_BLK), 0)
        rb = lax.broadcasted_iota(jnp.int32, (SB_BLK, SB_BLK), 1)
        upper = jnp.where(rb > ra, 1.0, 0.0).astype(BF16)
        upper2 = jnp.concatenate([upper, upper], axis=1)
        if masked:
            key_i = lax.broadcasted_iota(jnp.int32, (t, t), 0)
            qry_i = lax.broadcasted_iota(jnp.int32, (t, t), 1)
            strict = key_i < qry_i
        def logits(h):
            g = (h * SB_HD) // 128
            return _nn(k[:, g * 128:(g + 1) * 128], qz_sc[h])

        z_next = logits(0)
        for h in range(SB_HEADS):
            z = z_next
            if h + 1 < SB_HEADS:
                z_next = logits(h + 1)
            neg_abs = lax.bitcast_convert_type(
                lax.bitcast_convert_type(z, jnp.uint32) | jnp.uint32(0x80000000), F32)
            ls_pos = jnp.minimum(z, 0.0) - jnp.log(1.0 + jnp.exp2(neg_abs)) * LOG2E
            keep = ls_pos - z
            if masked:
                keep = jnp.where(strict, keep, 0.0)
            hi = keep.astype(BF16)
            lo = (keep - hi.astype(F32)).astype(BF16)
            carry = c_sc[h]
            parts = []
            for blk in range(t // SB_BLK - 1, -1, -1):
                sl = slice(blk * SB_BLK, (blk + 1) * SB_BLK)
                suffix = _nn(upper2, jnp.concatenate([hi[sl], lo[sl]], axis=0))
                parts.append(suffix + carry)
                carry = carry + suffix[0:1] + keep[blk * SB_BLK:blk * SB_BLK + 1]
            c_sc[h] = carry
            later = jnp.concatenate(parts[::-1], axis=0)
            wgt = jnp.exp2(ls_pos + later)
            if masked:
                wgt = jnp.where(strict, wgt, 0.0)
            acc_sc[h] = acc_sc[h] + _nn(vT[h * SB_HD:(h + 1) * SB_HD], wgt.astype(BF16))

    @pl.when(j == 0)
    def _():
        step(True)

    @pl.when(jnp.logical_and(j > 0, j <= i))
    def _():
        step(False)

    @pl.when(j == i)
    def _():
        o_ref[0] = jnp.concatenate([acc_sc[h] for h in range(SB_HEADS)], axis=0).T.astype(o_ref.dtype)


def sb_attention(qcT, kc, vcT, t):
    b, s, w = kc.shape
    n = s // t
    qi_tab, kj_tab = _causal_pairs(n, descending=True)
    qspec = pl.BlockSpec((1, w, t), lambda bi, p, qi, kj: (bi, 0, qi[p]))
    kspec = pl.BlockSpec((1, t, w), lambda bi, p, qi, kj: (bi, kj[p], 0))
    vspec = pl.BlockSpec((1, w, t), lambda bi, p, qi, kj: (bi, 0, kj[p]))
    return pl.pallas_call(
        functools.partial(_sb_kernel, t=t),
        grid_spec=pltpu.PrefetchScalarGridSpec(
            num_scalar_prefetch=2,
            grid=(b, qi_tab.shape[0]),
            in_specs=[qspec, kspec, vspec],
            out_specs=pl.BlockSpec((1, t, w), lambda bi, p, qi, kj: (bi, qi[p], 0)),
            scratch_shapes=[pltpu.VMEM((SB_HEADS, 128, t), BF16),
                            pltpu.VMEM((SB_HEADS, 1, t), F32), pltpu.VMEM((SB_HEADS, SB_HD, t), F32)]),
        out_shape=jax.ShapeDtypeStruct((b, s, w), BF16),
        compiler_params=_cparams(("arbitrary", "arbitrary")),
        name="sb_attention",
    )(qi_tab, kj_tab, qcT, kc, vcT)


def _dsa_kernel(qdT_ref, qiT_ref, wiT_ref, kdki_ref, vdT_ref, o_ref,
                key_sc, half_sc, qz_sc, qiz_sc, tie_sc, m_sc, acc_sc, *, t, topk):
    i = pl.program_id(1)
    ntile = i + 1
    wiT = wiT_ref[0]

    qdT = qdT_ref[0].astype(F32)
    zeros64 = jnp.zeros((128 - DSA_HD, t), F32)
    for h in range(DSA_HEADS):
        qz_sc[h] = jnp.concatenate([qdT[h * DSA_HD:(h + 1) * DSA_HD], zeros64], axis=0).astype(BF16)
    qiT = qiT_ref[0].astype(F32)
    for h in range(IDX_HEADS):
        qiz_sc[h] = jnp.concatenate(
            [jnp.zeros((DSA_HD, t), F32), qiT[h * IDX_HD:(h + 1) * IDX_HD],
             jnp.zeros((128 - DSA_HD - IDX_HD, t), F32)], axis=0).astype(BF16)

    def key_rows(jt):
        return kdki_ref[0, pl.ds(pl.multiple_of(jt * t, t), t), :]

    def score_keys(jt):
        kk = key_rows(jt)
        acc = jnp.zeros((t, t), F32)
        for h in range(IDX_HEADS):
            acc = acc + jnp.maximum(_nn(kk, qiz_sc[h]), 0.0) * wiT[h:h + 1]
        bits = lax.bitcast_convert_type(acc, jnp.int32)
        key = jnp.where(bits < 0, bits ^ jnp.int32(0x7FFFFFFF), bits)
        return jnp.where(acc == 0.0, 0, key)

    def p1(jt, carry):
        key_sc[jt] = score_keys(jt)
        return carry
    lax.fori_loop(0, i, p1, 0)
    key_i = lax.broadcasted_iota(jnp.int32, (t, t), 0)
    qry_i = lax.broadcasted_iota(jnp.int32, (t, t), 1)
    key_sc[i] = jnp.where(key_i <= qry_i, score_keys(i), INT_MIN)

    def count_ge(cand):
        def body(jt, part):
            ge = jnp.where(key_sc[jt] >= cand, 1, 0)
            return part + jnp.sum(ge.reshape(t // 8, 8, t), axis=0)
        part = lax.fori_loop(0, ntile, body, jnp.zeros((8, t), jnp.int32))
        return jnp.sum(part, axis=0, keepdims=True)

    def count16_ge(cand):
        cand16 = cand.astype(jnp.int16)

        def body(jt, acc):
            ge = jnp.where(half_sc[jt] >= cand16, jnp.bfloat16(1), jnp.bfloat16(0))
            g3 = ge.reshape(t // 16, 16, t)
            part = g3[0]
            for r in range(1, t // 16):
                part = part + g3[r]
            return acc + part.astype(F32)
        acc = lax.fori_loop(0, ntile, body, jnp.zeros((16, t), F32))
        return jnp.sum(acc, axis=0, keepdims=True)

    def search16(quota):
        def bis(it, thr):
            cand = thr + lax.shift_left(jnp.int32(1), jnp.int32(15) - it)
            return jnp.where(count16_ge(cand) >= quota, cand, thr)
        return lax.fori_loop(0, 16, bis, jnp.full((1, t), I16_MIN, jnp.int32))

    def fill_hi(jt, carry):
        half_sc[jt] = lax.shift_right_arithmetic(key_sc[jt], 16).astype(jnp.int16)
        return carry
    lax.fori_loop(0, ntile, fill_hi, 0)
    thr_hi = search16(jnp.float32(topk))
    above = jnp.where(thr_hi < I16_MAX, count16_ge(jnp.minimum(thr_hi + 1, I16_MAX)), 0.0)

    def fill_lo(jt, carry):
        key = key_sc[jt]
        low = (key & 0xFFFF) + I16_MIN
        in_group = lax.shift_right_arithmetic(key, 16) == thr_hi
        half_sc[jt] = jnp.where(in_group, low, I16_MIN).astype(jnp.int16)
        return carry
    lax.fori_loop(0, ntile, fill_lo, 0)
    thr_lo = search16(topk - above)
    thr = lax.shift_left(thr_hi, 16) | (thr_lo - I16_MIN)
    thr = jnp.maximum(thr, INT_MIN + 1)
    n_ge = count_ge(thr)

    @pl.when(jnp.max(n_ge) > topk)
    def _():
        n_gt = count_ge(thr + 1)
        quota = (topk - n_gt).astype(F32)
        tie_sc[...] = jnp.zeros(tie_sc.shape, F32)
        lower = jnp.where(qry_i < key_i, 1.0, 0.0).astype(BF16)

        def fix(jt, carry):
            key = key_sc[jt]
            eq = key == thr
            eqf = jnp.where(eq, 1.0, 0.0)
            rank = _nn(lower, eqf.astype(BF16)) + tie_sc[...]
            drop = jnp.logical_and(eq, rank >= quota)
            key_sc[jt] = jnp.where(drop, INT_MIN, key)
            tie_sc[...] = tie_sc[...] + jnp.sum(eqf, axis=0, keepdims=True)
            return carry
        lax.fori_loop(0, ntile, fix, 0)

    m_sc[...] = jnp.full(m_sc.shape, NEG_BIG, F32)
    acc_sc[...] = jnp.zeros(acc_sc.shape, F32)

    def p3(jt, carry):
        sel = key_sc[jt] >= thr
        kk = key_rows(jt)
        vT = jnp.concatenate([vdT_ref[0, jt], jnp.ones((ONES_ROWS, t), BF16)], axis=0)
        s_next = _nn(kk, qz_sc[0])
        for h in range(DSA_HEADS):
            s = jnp.where(sel, s_next, NEG_BIG)
            if h + 1 < DSA_HEADS:
                s_next = _nn(kk, qz_sc[h + 1])
            m_prev = m_sc[h]
            m_cur = jnp.maximum(m_prev, jnp.max(s, axis=0, keepdims=True))
            alpha = jnp.exp2(m_prev - m_cur)
            p = jnp.exp2(s - m_cur)
            acc_sc[h] = alpha * acc_sc[h] + _nn(vT, p.astype(BF16))
            m_sc[h] = m_cur
        return carry
    lax.fori_loop(0, ntile, p3, 0)
    o_ref[0] = jnp.concatenate(
        [acc_sc[h][0:DSA_HD] / acc_sc[h][DSA_HD:DSA_HD + 1] for h in range(DSA_HEADS)],
        axis=0).T.astype(o_ref.dtype)


def dsa_attention(qdT, qiT, wiT, kdki, vdT, t):
    b, w, s = qdT.shape
    n = s // t
    topk = min(TOPK_MAX, s // 4)
    qspec = pl.BlockSpec((1, w, t), lambda bi, i: (bi, 0, i))
    return pl.pallas_call(
        functools.partial(_dsa_kernel, t=t, topk=topk),
        grid=(b, n),
        in_specs=[qspec, qspec, pl.BlockSpec((1, IDX_HEADS, t), lambda bi, i: (bi, 0, i)),
                  pl.BlockSpec((1, s, 128), lambda bi, i: (bi, 0, 0)),
                  pl.BlockSpec((1, n, DSA_HD, t), lambda bi, i: (bi, 0, 0, 0))],
        out_specs=pl.BlockSpec((1, t, w), lambda bi, i: (bi, i, 0)),
        out_shape=jax.ShapeDtypeStruct((b, s, w), BF16),
        scratch_shapes=[pltpu.VMEM((n, t, t), jnp.int32), pltpu.VMEM((n, t, t), jnp.int16),
                        pltpu.VMEM((DSA_HEADS, 128, t), BF16), pltpu.VMEM((IDX_HEADS, 128, t), BF16),
                        pltpu.VMEM((1, t), F32),
                        pltpu.VMEM((DSA_HEADS, 1, t), F32),
                        pltpu.VMEM((DSA_HEADS, DSA_HD + ONES_ROWS, t), F32)],
        compiler_params=_cparams(("arbitrary", "arbitrary")),
        name="dsa_attention",
    )(qdT, qiT, wiT, kdki, vdT)


def _merge_kernel(oa_ref, ob_ref, oc_ref, od_ref, wg_ref, wb_ref, wo_ref, x_ref, mod_ref, pre_g_ref, pg_ref,
                  o_ref):
    d = x_ref.shape[2]
    x = x_ref[0]
    mod = mod_ref[0]
    y0 = x * lax.rsqrt(jnp.mean(x * x, axis=-1, keepdims=True) + EPS) * pre_g_ref[...]
    h = (y0 * (1.0 + mod[1:2]) + mod[0:1]).astype(BF16)

    def gate_logits(bi_):
        return _nn(h, wg_ref[:, bi_ * d:(bi_ + 1) * d])

    merged = None
    nxt = gate_logits(0)
    for bi_, o_r in enumerate((oa_ref, ob_ref, oc_ref, od_ref)):
        logit = nxt
        if bi_ + 1 < N_BRANCH:
            nxt = gate_logits(bi_ + 1)
        term = jax.nn.sigmoid(logit) * _nn(o_r[0], wb_ref[bi_])
        merged = term if merged is None else merged + term
    y = _nn(merged.astype(BF16), wo_ref[...])
    yn = y * lax.rsqrt(jnp.mean(y * y, axis=-1, keepdims=True) + EPS) * pg_ref[...]
    o_ref[0] = x + mod[2:3] * yn


def merge_out(oa, ob, oc, od, w_gates, w_branch, w_out, x, mod6, pre_g, post_g, tm=512):
    b, s, d = x.shape
    tm = min(tm, s)
    ospec = pl.BlockSpec((1, tm, BRANCH_W), lambda bi, i: (bi, i, 0))
    resident = lambda shape: pl.BlockSpec(shape, lambda bi, i: (0,) * len(shape), pipeline_mode=pl.Buffered(1))
    return pl.pallas_call(
        _merge_kernel,
        grid=(b, s // tm),
        in_specs=[ospec, ospec, ospec, ospec,
                  resident((d, N_BRANCH * d)),
                  resident((N_BRANCH, BRANCH_W, d)),
                  resident((d, d)),
                  pl.BlockSpec((1, tm, d), lambda bi, i: (bi, i, 0)),
                  pl.BlockSpec((1, 6, d), lambda bi, i: (bi, 0, 0)),
                  pl.BlockSpec((1, d), lambda bi, i: (0, 0)),
                  pl.BlockSpec((1, d), lambda bi, i: (0, 0))],
        out_specs=pl.BlockSpec((1, tm, d), lambda bi, i: (bi, i, 0)),
        out_shape=jax.ShapeDtypeStruct((b, s, d), F32),
        compiler_params=_cparams(("arbitrary", "arbitrary")),
        name="merge_out",
    )(oa, ob, oc, od, w_gates, w_branch, w_out, x, mod6, pre_g.reshape(1, d), post_g.reshape(1, d))


FFN_HALO = 16


FFN_SUB = 256


def _ffn_kernel(x_ref, halo_ref, mod_ref, pre_g_ref, wup_ref, cw_ref, cb_ref, wd_ref, post_g_ref, o_ref,
                act_sc, *, tm, dff):
    i = pl.program_id(1)
    mod = mod_ref[0]

    def modulated(xv):
        ms = jnp.mean(xv * xv, axis=-1, keepdims=True)
        y = xv * lax.rsqrt(ms + EPS) * pre_g_ref[...]
        return y * (1.0 + mod[4:5]) + mod[3:4]

    x = x_ref[0]
    halo = jnp.where(i == 0, 0.0, modulated(halo_ref[0]))
    hext = jnp.concatenate([halo.astype(BF16), modulated(x).astype(BF16)], axis=0)

    def up(col):
        return _nn(hext, wup_ref[:, col:col + FFN_SUB])

    def conv(u, col):
        cw = cw_ref[:, col:col + FFN_SUB]
        out = cb_ref[:, col:col + FFN_SUB] + cw[2:3] * u[FFN_HALO:FFN_HALO + tm]
        out = out + cw[1:2] * u[FFN_HALO - 1:FFN_HALO - 1 + tm]
        return out + cw[0:1] * u[FFN_HALO - 2:FFN_HALO - 2 + tm]

    nsub = dff // FFN_SUB
    nxt = (up(0), up(dff))
    for sb in range(nsub):
        ug, uv = nxt
        if sb + 1 < nsub:
            nxt = (up((sb + 1) * FFN_SUB), up(dff + (sb + 1) * FFN_SUB))
        gate = conv(ug, sb * FFN_SUB)
        val = conv(uv, dff + sb * FFN_SUB)
        act_sc[:, sb * FFN_SUB:(sb + 1) * FFN_SUB] = (gate * jax.nn.sigmoid(gate) * val).astype(BF16)
    y = _nn(act_sc[...], wd_ref[...])
    yn = y * lax.rsqrt(jnp.mean(y * y, axis=-1, keepdims=True) + EPS) * post_g_ref[...]
    o_ref[0] = x + mod[5:6] * yn


def conv_ffn(x, mod6, pre_g, w_up, conv_w, conv_b, w_down, post_g, tm=512):
    b, s, d = x.shape
    dff = w_down.shape[0]
    tm = min(tm, s)
    r = tm // FFN_HALO
    resident = lambda shape: pl.BlockSpec(shape, lambda bi, i: (0,) * len(shape), pipeline_mode=pl.Buffered(1))
    return pl.pallas_call(
        functools.partial(_ffn_kernel, tm=tm, dff=dff),
        grid=(b, s // tm),
        in_specs=[pl.BlockSpec((1, tm, d), lambda bi, i: (bi, i, 0)),
                  pl.BlockSpec((1, FFN_HALO, d), lambda bi, i: (bi, jnp.maximum(i * r - 1, 0), 0)),
                  pl.BlockSpec((1, 6, d), lambda bi, i: (bi, 0, 0)),
                  pl.BlockSpec((1, d), lambda bi, i: (0, 0)),
                  resident((d, 2 * dff)),
                  pl.BlockSpec((FFN_CONV_K, 2 * dff), lambda bi, i: (0, 0)),
                  pl.BlockSpec((1, 2 * dff), lambda bi, i: (0, 0)),
                  resident((dff, d)),
                  pl.BlockSpec((1, d), lambda bi, i: (0, 0))],
        out_specs=pl.BlockSpec((1, tm, d), lambda bi, i: (bi, i, 0)),
        out_shape=jax.ShapeDtypeStruct((b, s, d), F32),
        scratch_shapes=[pltpu.VMEM((tm, dff), BF16)],
        compiler_params=_cparams(("arbitrary", "arbitrary")),
        name="conv_ffn",
    )(x, x, mod6, pre_g.reshape(1, d), w_up, conv_w, conv_b.reshape(1, 2 * dff), w_down, post_g.reshape(1, d))


def _reorder_w_in(w_in_l):
    sizes = (512, 256, 256, 256, 256, 256, 256, 256, 64, 64, 256, 32, 8, GATES_W)
    offs = [0]
    for sz in sizes:
        offs.append(offs[-1] + sz)
    col = lambda idx: w_in_l[:, offs[idx]:offs[idx + 1]]
    d = w_in_l.shape[0]
    misc = jnp.concatenate([col(8), col(9), col(11), col(12), jnp.zeros((d, 256 - 168), w_in_l.dtype)], axis=1)
    main = jnp.concatenate([col(0), col(1), col(2), col(3), col(4), col(5), col(6), col(7), misc, col(10)],
                           axis=1)
    return main.astype(BF16), col(13).astype(BF16)


def kernel(x, c, positions, ada_w, ada_b, mix_pre_g, mix_post_g, ffn_pre_g, ffn_post_g, w_in, conv_a_w,
           conv_a_b, conv_a_ln_g, conv_a_ln_b, lam_q1, lam_k1, lam_q2, lam_k2, diff_subln_g, w_branch,
           w_out, w_up, ffn_conv_w, ffn_conv_b, w_down):
    depth = ada_w.shape[0]
    b, s, d = x.shape
    t = min(ATT_T, s)
    mod_all = adaln_mod(c, ada_w, ada_b)
    tabs = rope_tables(positions)
    for l in range(depth):
        lam_init = 0.8 - 0.6 * math.exp(-0.3 * l)
        mod6 = mod_all[l].reshape(b, 6, d)
        w_main, w_gates = _reorder_w_in(w_in[l])
        ua, qbT, kb, vbT, qcT, kc, vcT, qdT, qiT, kdki, vdT, wiT = in_proj(x, mod6, mix_pre_g[l], w_main, tabs, t)
        oa = conformer(ua, conv_a_w[l], conv_a_b[l], conv_a_ln_g[l], conv_a_ln_b[l])
        ob = diff_attention(qbT, kb, vbT, lam_q1[l], lam_k1[l], lam_q2[l], lam_k2[l], diff_subln_g[l],
                            lam_init, t)
        oc = sb_attention(qcT, kc, vcT, t)
        od = dsa_attention(qdT, qiT, wiT, kdki, vdT, t)
        x = merge_out(oa, ob, oc, od, w_gates, w_branch[l].astype(BF16), w_out[l].astype(BF16), x, mod6,
                      mix_pre_g[l], mix_post_g[l])
        x = conv_ffn(x, mod6, ffn_pre_g[l], w_up[l].astype(BF16), ffn_conv_w[l], ffn_conv_b[l],
                     w_down[l].astype(BF16), ffn_post_g[l])
    return x
```
